```python
import math
import jax, jax.numpy as jnp
from jax import lax
import numpy as np

D_MODEL = 2048
BATCH = 16
SEQ = 2048
DEPTH = 4

N_MEM = 256
HEAD_DIM = 128
DIFF_WIDTH = D_MODEL // 2
N_DIFF_HEADS = DIFF_WIDTH // HEAD_DIM
DIFF_QK_DIM = HEAD_DIM // 2
DIFF_QK_WIDTH = N_DIFF_HEADS * 2 * DIFF_QK_DIM
Q_BLOCK = 128
SGU_WIDTH = D_MODEL // 2
SGU_CHUNK = 128
SGU_GROUPS = SGU_WIDTH // HEAD_DIM
SGU_GROUP_DIM = SGU_WIDTH // SGU_GROUPS
CONV_WIDTH = D_MODEL // 2
CONV_TAPS = 3
N_BRANCHES = 3
MIX_SPLIT_SIZES = (DIFF_QK_WIDTH, DIFF_QK_WIDTH, DIFF_WIDTH,
                   SGU_WIDTH, SGU_WIDTH,
                   CONV_WIDTH, CONV_WIDTH, CONV_WIDTH,
                   N_BRANCHES * D_MODEL)
MIX_IN_WIDTH = sum(MIX_SPLIT_SIZES)
N_MEM_HEADS = 4
MEM_HEAD_DIM = 128
MEM_WIDTH = N_MEM_HEADS * MEM_HEAD_DIM
D_FF = ((8 * D_MODEL // 3 + 255) // 256) * 256
RMS_EPS = 1e-6

kernel_name = 'hybrid_diffattn_sgu_shortconv_macaron'


def rms_norm(x, g, eps=RMS_EPS):
    xf = x.astype(jnp.float32)
    y = xf * lax.rsqrt(jnp.mean(xf * xf, axis=-1, keepdims=True) + eps)
    return (y * g.astype(jnp.float32)).astype(x.dtype)


def swiglu(n, w_in, w_out):
    a, b = jnp.split(n @ w_in, 2, axis=-1)
    return (jax.nn.silu(a) * b) @ w_out


def alibi_slopes(n_heads):
    return 2.0 ** (-8.0 * jnp.arange(1, n_heads + 1, dtype=jnp.float32) / n_heads)


def diff_attention(q, k, v, lam, slopes):
    bsz, seq, n_heads, _, dq = q.shape
    n_blocks = seq // Q_BLOCK
    q = q * (1.0 / math.sqrt(dq))
    q_blocks = q.reshape(bsz, n_blocks, Q_BLOCK, n_heads, 2, dq).transpose(1, 0, 3, 4, 2, 5)
    k_t = k.transpose(0, 2, 3, 1, 4)
    v_t = v.transpose(0, 2, 1, 3)
    k_pos = jnp.arange(seq)

    def block(args):
        q_blk, start = args
        q_pos = start + jnp.arange(Q_BLOCK)
        dist = q_pos[:, None] - k_pos[None, :]
        bias = -slopes[:, None, None] * dist.astype(jnp.float32)
        s = jnp.einsum('bhiqd,bhikd->bhiqk', q_blk, k_t,
                       preferred_element_type=jnp.float32)
        s = jnp.where(dist >= 0, s + bias[:, None], -jnp.inf)
        p = jax.nn.softmax(s, axis=-1)
        w = p[:, :, 0] - lam * p[:, :, 1]
        return jnp.einsum('bhqk,bhkd->bhqd', w.astype(v_t.dtype), v_t)

    starts = jnp.arange(n_blocks) * Q_BLOCK
    o = lax.map(block, (q_blocks, starts))
    return o.transpose(1, 0, 3, 2, 4).reshape(bsz, seq, n_heads, v.shape[-1])


def hybrid_mixer(n, lam, lam_init, w_in, diff_subln, diff_w_out, sgu_norm, sgu_w_s, sgu_b,
                 sgu_w_out, conv_w, conv_w_out, w_o):
    bsz, seq, _ = n.shape
    offsets = np.cumsum(MIX_SPLIT_SIZES)[:-1].tolist()
    q, k, v, u, g_v, c_b, c_c, c_x, gate_logits = jnp.split(n @ w_in, offsets, axis=-1)

    q = q.reshape(bsz, seq, N_DIFF_HEADS, 2, DIFF_QK_DIM)
    k = k.reshape(bsz, seq, N_DIFF_HEADS, 2, DIFF_QK_DIM)
    v = v.reshape(bsz, seq, N_DIFF_HEADS, HEAD_DIM)
    o = diff_attention(q, k, v, lam, alibi_slopes(N_DIFF_HEADS))
    y_a = (rms_norm(o, diff_subln) * (1.0 - lam_init)).reshape(bsz, seq, DIFF_WIDTH)

    u = jax.nn.gelu(u, approximate=False)
    g_v = rms_norm(jax.nn.gelu(g_v, approximate=False), sgu_norm)
    g_v = g_v.reshape(bsz, seq // SGU_CHUNK, SGU_CHUNK, SGU_GROUPS, SGU_GROUP_DIM)
    w_s = jnp.tril(sgu_w_s)
    mixed = jnp.einsum('gts,bcsgd->bctgd', w_s, g_v) + sgu_b.T[None, None, :, :, None]
    y_b = u * mixed.reshape(bsz, seq, SGU_WIDTH)

    z = c_c * c_x
    zp = jnp.pad(z, ((0, 0), (CONV_TAPS - 1, 0), (0, 0)))
    conv = conv_w[0] * zp[:, 0:seq] + conv_w[1] * zp[:, 1:seq + 1] + conv_w[2] * zp[:, 2:seq + 2]
    y_c = c_b * conv

    gates = jax.nn.sigmoid(gate_logits.astype(jnp.float32)).astype(n.dtype)
    gates = gates.reshape(bsz, seq, N_BRANCHES, D_MODEL)
    merged = (gates[:, :, 0] * (y_a @ diff_w_out)
              + gates[:, :, 1] * (y_b @ sgu_w_out)
              + gates[:, :, 2] * (y_c @ conv_w_out))
    return merged @ w_o


def memory_attention(n, mem_n, w_q, w_kv, w_o):
    bsz, seq, _ = n.shape
    q = (n @ w_q).reshape(bsz, seq, N_MEM_HEADS, MEM_HEAD_DIM)
    kv = (mem_n @ w_kv).reshape(bsz, mem_n.shape[1], 2, N_MEM_HEADS, MEM_HEAD_DIM)
    k, v = kv[:, :, 0], kv[:, :, 1]
    s = jnp.einsum('bqhd,bkhd->bhqk', q, k, preferred_element_type=jnp.float32)
    p = jax.nn.softmax(s * (1.0 / math.sqrt(MEM_HEAD_DIM)), axis=-1)
    o = jnp.einsum('bhqk,bkhd->bqhd', p.astype(v.dtype), v).reshape(bsz, seq, MEM_WIDTH)
    return o @ w_o


def setup_inputs(seed: int = 0) -> dict:
    key = jax.random.key(seed)
    ks = iter(jax.random.split(key, 32))
    L = DEPTH

    def dense(shape, fan_in):
        return jax.random.normal(next(ks), shape, jnp.float32) * (fan_in ** -0.5)

    def gain(shape):
        return 1.0 + 0.02 * jax.random.normal(next(ks), shape, jnp.float32)

    return {
        'x': jax.random.normal(next(ks), (BATCH, SEQ, D_MODEL), jnp.float32),
        'mem': jax.random.normal(next(ks), (BATCH, N_MEM, D_MODEL), jnp.float32),
        'ffn1_norm': gain((L, D_MODEL)),
        'ffn1_w_in': dense((L, D_MODEL, 2 * D_FF), D_MODEL),
        'ffn1_w_out': dense((L, D_FF, D_MODEL), D_FF),
        'mix_norm': gain((L, D_MODEL)),
        'mix_w_in': dense((L, D_MODEL, MIX_IN_WIDTH), D_MODEL),
        'diff_lambda': 0.1 * jax.random.normal(next(ks), (L, 4, DIFF_QK_DIM), jnp.float32),
        'diff_subln': gain((L, HEAD_DIM)),
        'diff_w_out': dense((L, DIFF_WIDTH, D_MODEL), DIFF_WIDTH),
        'sgu_norm': gain((L, SGU_WIDTH)),
        'sgu_w_s': dense((L, SGU_GROUPS, SGU_CHUNK, SGU_CHUNK), SGU_CHUNK),
        'sgu_b': gain((L, SGU_GROUPS, SGU_CHUNK)),
        'sgu_w_out': dense((L, SGU_WIDTH, D_MODEL), SGU_WIDTH),
        'conv_w': dense((L, CONV_TAPS, CONV_WIDTH), CONV_TAPS),
        'conv_w_out': dense((L, CONV_WIDTH, D_MODEL), CONV_WIDTH),
        'mix_w_o': dense((L, D_MODEL, D_MODEL), D_MODEL),
        'xattn_norm': gain((L, D_MODEL)),
        'mem_norm': gain((L, D_MODEL)),
        'xattn_w_q': dense((L, D_MODEL, MEM_WIDTH), D_MODEL),
        'xattn_w_kv': dense((L, D_MODEL, 2 * MEM_WIDTH), D_MODEL),
        'xattn_w_o': dense((L, MEM_WIDTH, D_MODEL), MEM_WIDTH),
        'ffn2_norm': gain((L, D_MODEL)),
        'ffn2_w_in': dense((L, D_MODEL, 2 * D_FF), D_MODEL),
        'ffn2_w_out': dense((L, D_FF, D_MODEL), D_FF),
        'final_norm': gain((D_MODEL,)),
    }


def reference(x, mem, ffn1_norm, ffn1_w_in, ffn1_w_out, mix_norm, mix_w_in, diff_lambda,
              diff_subln, diff_w_out, sgu_norm, sgu_w_s, sgu_b, sgu_w_out, conv_w, conv_w_out,
              mix_w_o, xattn_norm, mem_norm, xattn_w_q, xattn_w_kv, xattn_w_o, ffn2_norm,
              ffn2_w_in, ffn2_w_out, final_norm):
    h = x
    for l in range(DEPTH):
        lam_init = 0.8 - 0.6 * math.exp(-0.3 * l)
        lp = diff_lambda[l].astype(jnp.float32)
        lam = jnp.exp(jnp.sum(lp[0] * lp[1])) - jnp.exp(jnp.sum(lp[2] * lp[3])) + lam_init

        h = h + 0.5 * swiglu(rms_norm(h, ffn1_norm[l]), ffn1_w_in[l], ffn1_w_out[l])
        h = h + hybrid_mixer(rms_norm(h, mix_norm[l]), lam, lam_init, mix_w_in[l], diff_subln[l],
                             diff_w_out[l], sgu_norm[l], sgu_w_s[l], sgu_b[l], sgu_w_out[l],
                             conv_w[l], conv_w_out[l], mix_w_o[l])
        h = h + memory_attention(rms_norm(h, xattn_norm[l]), rms_norm(mem, mem_norm[l]),
                                 xattn_w_q[l], xattn_w_kv[l], xattn_w_o[l])
        h = h + 0.5 * swiglu(rms_norm(h, ffn2_norm[l]), ffn2_w_in[l], ffn2_w_out[l])
    return rms_norm(h, final_norm)
```

```python
import functools
import math

import jax
import jax.numpy as jnp
from jax import lax
from jax.experimental import pallas as pl
from jax.experimental.pallas import tpu as pltpu

HEAD_DIM = 128
SGU_CHUNK = 128
N_MEM_HEADS = 4
N_BRANCHES = 3
CONV_TAPS = 3
RMS_EPS = 1e-6
NEG_BIG = -1e30
V7X_VMEM_LIMIT_BYTES = 56 * 1024 * 1024
NORM_ROWS = 256

BF16 = jnp.bfloat16
F32 = jnp.float32


def _params(*sem):
    return pltpu.CompilerParams(dimension_semantics=sem, vmem_limit_bytes=V7X_VMEM_LIMIT_BYTES)


def _pick(total, target):
    b = min(total, target)
    while total % b:
        b //= 2
    return b


def _rms_rows(x, g):
    ms = jnp.mean(x * x, axis=-1, keepdims=True)
    return x * lax.rsqrt(ms + RMS_EPS) * g


def _fill_norm(x_ref, g_ref, n_ref):
    bm = x_ref.shape[0]
    rows = min(NORM_ROWS, bm)
    g = g_ref[...]

    def body(c, carry):
        r0 = pl.multiple_of(c * rows, rows)
        n_ref[pl.ds(r0, rows), :] = _rms_rows(x_ref[pl.ds(r0, rows), :], g).astype(n_ref.dtype)
        return carry

    lax.fori_loop(0, bm // rows, body, 0)


def _dot(a, b):
    return jnp.dot(a, b, preferred_element_type=F32)


def _dot_t(a, b):
    return lax.dot_general(a, b, (((1,), (1,)), ((), ())), preferred_element_type=F32)


def _gelu(x):
    return 0.5 * x * (1.0 + lax.erf(x * (1.0 / math.sqrt(2.0))))


def _norm_mm_kernel(x_ref, g_ref, w_ref, o_ref, n_ref):
    @pl.when(pl.program_id(1) == 0)
    def _():
        _fill_norm(x_ref, g_ref, n_ref)

    o_ref[...] = _dot(n_ref[...], w_ref[...]).astype(o_ref.dtype)


def norm_matmul(x, gain, w, *, bm_target=1024, bn_target=1024):
    m, d = x.shape
    n = w.shape[1]
    bm, bn = _pick(m, bm_target), _pick(n, bn_target)
    return pl.pallas_call(
        _norm_mm_kernel,
        grid=(m // bm, n // bn),
        in_specs=[pl.BlockSpec((bm, d), lambda i, j: (i, 0)),
                  pl.BlockSpec((1, d), lambda i, j: (0, 0)),
                  pl.BlockSpec((d, bn), lambda i, j: (0, j))],
        out_specs=pl.BlockSpec((bm, bn), lambda i, j: (i, j)),
        out_shape=jax.ShapeDtypeStruct((m, n), BF16),
        scratch_shapes=[pltpu.VMEM((bm, d), BF16)],
        compiler_params=_params("parallel", "arbitrary"),
        name="norm_matmul",
    )(x, gain.reshape(1, d), w)


def _norm_swiglu_kernel(x_ref, g_ref, wa_ref, wb_ref, o_ref, n_ref):
    @pl.when(pl.program_id(1) == 0)
    def _():
        _fill_norm(x_ref, g_ref, n_ref)

    n = n_ref[...]
    a = _dot(n, wa_ref[...])
    b = _dot(n, wb_ref[...])
    o_ref[...] = (a * jax.nn.sigmoid(a) * b).astype(o_ref.dtype)


def norm_swiglu_in(x, gain, w_in, *, bm_target=1024, bn_target=512):
    m, d = x.shape
    f = w_in.shape[1] // 2
    bm, bn = _pick(m, bm_target), _pick(f, bn_target)
    nb = f // bn
    return pl.pallas_call(
        _norm_swiglu_kernel,
        grid=(m // bm, nb),
        in_specs=[pl.BlockSpec((bm, d), lambda i, j: (i, 0)),
                  pl.BlockSpec((1, d), lambda i, j: (0, 0)),
                  pl.BlockSpec((d, bn), lambda i, j: (0, j)),
                  pl.BlockSpec((d, bn), lambda i, j: (0, j + nb))],
        out_specs=pl.BlockSpec((bm, bn), lambda i, j: (i, j)),
        out_shape=jax.ShapeDtypeStruct((m, f), BF16),
        scratch_shapes=[pltpu.VMEM((bm, d), BF16)],
        compiler_params=_params("parallel", "arbitrary"),
        name="norm_swiglu_in",
    )(x, gain.reshape(1, d), w_in, w_in)


def _mm_res_kernel(x_ref, w_ref, r_ref, o_ref, *, scale):
    o_ref[...] = r_ref[...] + scale * _dot(x_ref[...], w_ref[...])


def matmul_residual(x, w, res, scale, *, bm_target=512, bn_target=1024):
    m, k = x.shape
    n = w.shape[1]
    bm, bn = _pick(m, bm_target), _pick(n, bn_target)
    return pl.pallas_call(
        functools.partial(_mm_res_kernel, scale=scale),
        grid=(n // bn, m // bm),
        in_specs=[pl.BlockSpec((bm, k), lambda j, i: (i, 0)),
                  pl.BlockSpec((k, bn), lambda j, i: (0, j)),
                  pl.BlockSpec((bm, bn), lambda j, i: (i, j))],
        out_specs=pl.BlockSpec((bm, bn), lambda j, i: (i, j)),
        out_shape=jax.ShapeDtypeStruct((m, n), F32),
        compiler_params=_params("parallel", "parallel"),
        name="matmul_residual",
    )(x, w, res)


def _mix_in_kernel(x_ref, g_ref, w_ref, sgun_ref, ws_ref, sb_ref, cw_ref,
                   qkv_ref, yb_ref, yc_ref,
                   n_ref, u_ref, gv_ref, cb_ref, cc_ref, carry_ref, *, blocks_per_seq):
    i, j = pl.program_id(0), pl.program_id(1)
    bm, width = u_ref.shape
    n_groups = width // HEAD_DIM
    n_chunks = bm // SGU_CHUNK

    @pl.when(j == 0)
    def _():
        _fill_norm(x_ref, g_ref, n_ref)

    acc = _dot(n_ref[...], w_ref[...])

    @pl.when(j < 3)
    def _():
        qkv_ref[...] = acc.astype(qkv_ref.dtype)

    @pl.when(j == 3)
    def _():
        u_ref[...] = _gelu(acc)

    @pl.when(j == 4)
    def _():
        gv_ref[...] = _rms_rows(_gelu(acc), sgun_ref[...]).astype(gv_ref.dtype)
        t_idx = lax.broadcasted_iota(jnp.int32, (SGU_CHUNK, SGU_CHUNK), 0)
        s_idx = lax.broadcasted_iota(jnp.int32, (SGU_CHUNK, SGU_CHUNK), 1)
        for g in range(n_groups):
            cols = slice(g * HEAD_DIM, (g + 1) * HEAD_DIM)
            w_tril = jnp.where(s_idx <= t_idx, ws_ref[g], 0.0).astype(BF16)
            xg = jnp.concatenate(
                [gv_ref[c * SGU_CHUNK:(c + 1) * SGU_CHUNK, cols] for c in range(n_chunks)], axis=1)
            mixed = _dot(w_tril, xg) + sb_ref[g]
            for c in range(n_chunks):
                rows = slice(c * SGU_CHUNK, (c + 1) * SGU_CHUNK)
                yb_ref[rows, cols] = (
                    u_ref[rows, cols] * mixed[:, c * HEAD_DIM:(c + 1) * HEAD_DIM]).astype(yb_ref.dtype)

    @pl.when(j == 5)
    def _():
        cb_ref[...] = acc

    @pl.when(j == 6)
    def _():
        cc_ref[...] = acc

    @pl.when(j == 7)
    def _():
        @pl.when(i % blocks_per_seq == 0)
        def _():
            carry_ref[...] = jnp.zeros_like(carry_ref)

        z = cc_ref[...] * acc
        prev1 = carry_ref[7:8, :]
        prev2 = carry_ref[6:7, :]
        row = lax.broadcasted_iota(jnp.int32, z.shape, 0)
        z1 = jnp.where(row == 0, prev1, pltpu.roll(z, 1, 0))
        z2 = jnp.where(row == 0, prev2, jnp.where(row == 1, prev1, pltpu.roll(z, 2, 0)))
        conv = cw_ref[0:1, :] * z2 + cw_ref[1:2, :] * z1 + cw_ref[2:3, :] * z
        yc_ref[...] = (cb_ref[...] * conv).astype(yc_ref.dtype)
        carry_ref[...] = z[bm - 8:bm, :]


def mixer_in(x, gain, w, sgu_norm, sgu_w_s, sgu_b, conv_w, *, seq, bm_target=512):
    m, d = x.shape
    width = d // 2
    assert w.shape == (d, 8 * width)
    bm = _pick(seq, bm_target)
    assert bm % SGU_CHUNK == 0 and bm >= 8
    n_groups = width // HEAD_DIM
    const2 = lambda i, j: (0, 0)
    const3 = lambda i, j: (0, 0, 0)
    row_blk = lambda i, j: (i, 0)
    return pl.pallas_call(
        functools.partial(_mix_in_kernel, blocks_per_seq=seq // bm),
        grid=(m // bm, 8),
        in_specs=[pl.BlockSpec((bm, d), row_blk),
                  pl.BlockSpec((1, d), const2),
                  pl.BlockSpec((d, width), lambda i, j: (0, j)),
                  pl.BlockSpec((1, width), const2),
                  pl.BlockSpec((n_groups, SGU_CHUNK, SGU_CHUNK), const3),
                  pl.BlockSpec((n_groups, SGU_CHUNK, 1), const3),
                  pl.BlockSpec((CONV_TAPS, width), const2)],
        out_specs=[pl.BlockSpec((bm, width), lambda i, j: (i, jnp.minimum(j, 2))),
                   pl.BlockSpec((bm, width), row_blk),
                   pl.BlockSpec((bm, width), row_blk)],
        out_shape=[jax.ShapeDtypeStruct((m, 3 * width), BF16),
                   jax.ShapeDtypeStruct((m, width), BF16),
                   jax.ShapeDtypeStruct((m, width), BF16)],
        scratch_shapes=[pltpu.VMEM((bm, d), BF16),
                        pltpu.VMEM((bm, width), F32),
                        pltpu.VMEM((bm, width), BF16),
                        pltpu.VMEM((bm, width), F32),
                        pltpu.VMEM((bm, width), F32),
                        pltpu.VMEM((8, width), F32)],
        compiler_params=_params("arbitrary", "arbitrary"),
        name="mixer_in",
    )(x, gain.reshape(1, d), w, sgu_norm.reshape(1, width), sgu_w_s,
      sgu_b.reshape(n_groups, SGU_CHUNK, 1), conv_w)


def _diff_attn_kernel(scal_ref, q_ref, k_ref, v_ref, lam_ref, sub_ref, o_ref,
                      m_ref, l_ref, acc_ref, *, n_heads, tq):
    h, qi = pl.program_id(1), pl.program_id(2)
    slope = scal_ref[h]
    lam_init = scal_ref[n_heads]
    half = HEAD_DIM // 2

    q = q_ref[0].astype(F32) * (1.0 / math.sqrt(half))
    lane = lax.broadcasted_iota(jnp.int32, q.shape, 1)
    qq = jnp.concatenate([jnp.where(lane < half, q, 0.0),
                          jnp.where(lane >= half, q, 0.0)], axis=0).astype(BF16)

    r_idx = lax.broadcasted_iota(jnp.int32, (tq, tq), 0)
    c_idx = lax.broadcasted_iota(jnp.int32, (tq, tq), 1)
    rel = (c_idx - r_idx).astype(F32)
    causal = jnp.concatenate([c_idx <= r_idx, c_idx <= r_idx], axis=0)

    m_ref[...] = jnp.full_like(m_ref, NEG_BIG)
    l_ref[...] = jnp.zeros_like(l_ref)
    acc_ref[...] = jnp.zeros_like(acc_ref)

    def step(j, masked):
        k0 = pl.multiple_of(j * tq, tq)
        kj = k_ref[0, pl.ds(k0, tq), :]
        vj = v_ref[0, pl.ds(k0, tq), :]
        s = _dot_t(qq, kj)
        bias = slope * (rel - ((qi - j) * tq).astype(F32))
        t = s + jnp.concatenate([bias, bias], axis=0)
        if masked:
            t = jnp.where(causal, t, NEG_BIG)
        m_old = m_ref[...]
        m_new = jnp.maximum(m_old, jnp.max(t, axis=-1, keepdims=True))
        p = jnp.exp(t - m_new)
        alpha = jnp.exp(m_old - m_new)
        l_ref[...] = alpha * l_ref[...] + jnp.sum(p, axis=-1, keepdims=True)
        acc_ref[...] = alpha * acc_ref[...] + _dot(p.astype(BF16), vj)
        m_ref[...] = m_new

    def body(j, carry):
        step(j, False)
        return carry

    lax.fori_loop(0, qi, body, 0)
    step(qi, True)

    a = acc_ref[...] / l_ref[...]
    lp = lam_ref[...]
    lam = (jnp.exp(jnp.sum(lp[0:1] * lp[1:2], axis=-1, keepdims=True))
           - jnp.exp(jnp.sum(lp[2:3] * lp[3:4], axis=-1, keepdims=True)) + lam_init)
    o = a[:tq] - lam * a[tq:]
    o_ref[0] = (_rms_rows(o, sub_ref[...]) * (1.0 - lam_init)).astype(o_ref.dtype)


def diff_attention(qkv, lam_params, subln, lam_init, *, batch, seq, tq_target=256):
    width = qkv.shape[1] // 3
    n_heads = width // HEAD_DIM
    tq = _pick(seq, tq_target)
    slopes = 2.0 ** (-8.0 * jnp.arange(1, n_heads + 1, dtype=F32) / n_heads)
    scal = jnp.concatenate([slopes, jnp.full((1,), lam_init, F32)])
    qkv3 = qkv.reshape(batch, seq, 3 * width)
    out = pl.pallas_call(
        functools.partial(_diff_attn_kernel, n_heads=n_heads, tq=tq),
        grid=(batch, n_heads, seq // tq),
        in_specs=[pl.BlockSpec(memory_space=pltpu.SMEM),
                  pl.BlockSpec((1, tq, HEAD_DIM), lambda b, h, i: (b, i, h)),
                  pl.BlockSpec((1, seq, HEAD_DIM), lambda b, h, i: (b, 0, n_heads + h)),
                  pl.BlockSpec((1, seq, HEAD_DIM), lambda b, h, i: (b, 0, 2 * n_heads + h)),
                  pl.BlockSpec(lam_params.shape, lambda b, h, i: (0, 0)),
                  pl.BlockSpec((1, HEAD_DIM), lambda b, h, i: (0, 0))],
        out_specs=pl.BlockSpec((1, tq, HEAD_DIM), lambda b, h, i: (b, i, h)),
        out_shape=jax.ShapeDtypeStruct((batch, seq, width), BF16),
        scratch_shapes=[pltpu.VMEM((2 * tq, 1), F32),
                        pltpu.VMEM((2 * tq, 1), F32),
                        pltpu.VMEM((2 * tq, HEAD_DIM), F32)],
        compiler_params=_params("parallel", "parallel", "arbitrary"),
        name="diff_attention",
    )(scal, qkv3, qkv3, qkv3, lam_params, subln.reshape(1, HEAD_DIM))
    return out.reshape(batch * seq, width)


def _merge_kernel(x_ref, g_ref, wg0_ref, wg1_ref, wg2_ref, ya_ref, yb_ref, yc_ref,
                  wa_ref, wb_ref, wc_ref, o_ref, n_ref):
    @pl.when(pl.program_id(1) == 0)
    def _():
        _fill_norm(x_ref, g_ref, n_ref)

    n = n_ref[...]

    def branch(wg_ref, y_ref, w_ref):
        return jax.nn.sigmoid(_dot(n, wg_ref[...])) * _dot(y_ref[...], w_ref[...])

    merged = branch(wg0_ref, ya_ref, wa_ref) + branch(wg1_ref, yb_ref, wb_ref)
    o_ref[...] = (merged + branch(wg2_ref, yc_ref, wc_ref)).astype(o_ref.dtype)


def gated_merge(x, gain, w_gate, y_a, y_b, y_c, w_a, w_b, w_c, *, bm_target=512, bn_target=512):
    m, d = x.shape
    width = y_a.shape[1]
    bm, bn = _pick(m, bm_target), _pick(d, bn_target)
    nb = d // bn
    row_blk = lambda i, j: (i, 0)
    col_blk = lambda i, j: (0, j)
    return pl.pallas_call(
        _merge_kernel,
        grid=(m // bm, nb),
        in_specs=[pl.BlockSpec((bm, d), row_blk),
                  pl.BlockSpec((1, d), lambda i, j: (0, 0)),
                  pl.BlockSpec((d, bn), col_blk),
                  pl.BlockSpec((d, bn), lambda i, j: (0, j + nb)),
                  pl.BlockSpec((d, bn), lambda i, j: (0, j + 2 * nb)),
                  pl.BlockSpec((bm, width), row_blk),
                  pl.BlockSpec((bm, width), row_blk),
                  pl.BlockSpec((bm, width), row_blk),
                  pl.BlockSpec((width, bn), col_blk),
                  pl.BlockSpec((width, bn), col_blk),
                  pl.BlockSpec((width, bn), col_blk)],
        out_specs=pl.BlockSpec((bm, bn), lambda i, j: (i, j)),
        out_shape=jax.ShapeDtypeStruct((m, d), BF16),
        scratch_shapes=[pltpu.VMEM((bm, d), BF16)],
        compiler_params=_params("parallel", "arbitrary"),
        name="gated_merge",
    )(x, gain.reshape(1, d), w_gate, w_gate, w_gate, y_a, y_b, y_c, w_a, w_b, w_c)


def _mem_attn_kernel(x_ref, g_ref, wq_ref, kv_ref, wo_ref, o_ref, n_ref):
    mem_width = wq_ref.shape[1]
    _fill_norm(x_ref, g_ref, n_ref)
    q = _dot(n_ref[...], wq_ref[...]).astype(BF16)
    kv = kv_ref[0]
    scale = 1.0 / math.sqrt(HEAD_DIM)
    heads = []
    for hd in range(mem_width // HEAD_DIM):
        cols = slice(hd * HEAD_DIM, (hd + 1) * HEAD_DIM)
        k_h = kv[:, hd * HEAD_DIM:(hd + 1) * HEAD_DIM]
        v_h = kv[:, mem_width + hd * HEAD_DIM:mem_width + (hd + 1) * HEAD_DIM]
        s = _dot_t(q[:, cols], k_h) * scale
        p = jnp.exp(s - jnp.max(s, axis=-1, keepdims=True))
        denom = jnp.sum(p, axis=-1, keepdims=True)
        heads.append(_dot(p.astype(BF16), v_h) / denom)
    o = jnp.concatenate(heads, axis=1).astype(BF16)
    o_ref[...] = x_ref[...] + _dot(o, wo_ref[...])


def memory_attention(x, gain, w_q, kv, w_o, *, seq, bm_target=512):
    m, d = x.shape
    mem_width = w_q.shape[1]
    bm = _pick(seq, bm_target)
    per_seq = seq // bm
    return pl.pallas_call(
        _mem_attn_kernel,
        grid=(m // bm,),
        in_specs=[pl.BlockSpec((bm, d), lambda i: (i, 0)),
                  pl.BlockSpec((1, d), lambda i: (0, 0)),
                  pl.BlockSpec((d, mem_width), lambda i: (0, 0)),
                  pl.BlockSpec((1,) + kv.shape[1:], lambda i: (i // per_seq, 0, 0)),
                  pl.BlockSpec((mem_width, d), lambda i: (0, 0))],
        out_specs=pl.BlockSpec((bm, d), lambda i: (i, 0)),
        out_shape=jax.ShapeDtypeStruct((m, d), F32),
        scratch_shapes=[pltpu.VMEM((bm, d), BF16)],
        compiler_params=_params("parallel"),
        name="memory_attention",
    )(x, gain.reshape(1, d), w_q, kv, w_o)


def _final_norm_kernel(x_ref, g_ref, o_ref):
    o_ref[...] = _rms_rows(x_ref[...], g_ref[...])


def output_norm(x, gain, *, bm_target=256):
    m, d = x.shape
    bm = _pick(m, bm_target)
    return pl.pallas_call(
        _final_norm_kernel,
        grid=(m // bm,),
        in_specs=[pl.BlockSpec((bm, d), lambda i: (i, 0)),
                  pl.BlockSpec((1, d), lambda i: (0, 0))],
        out_specs=pl.BlockSpec((bm, d), lambda i: (i, 0)),
        out_shape=jax.ShapeDtypeStruct((m, d), F32),
        compiler_params=_params("parallel"),
        name="final_norm",
    )(x, gain.reshape(1, d))


def kernel(x, mem, ffn1_norm, ffn1_w_in, ffn1_w_out, mix_norm, mix_w_in, diff_lambda, diff_subln, diff_w_out, sgu_norm, sgu_w_s, sgu_b, sgu_w_out, conv_w, conv_w_out, mix_w_o, xattn_norm, mem_norm, xattn_w_q, xattn_w_kv, xattn_w_o, ffn2_norm, ffn2_w_in, ffn2_w_out, final_norm):
    batch, seq, d = x.shape
    n_mem = mem.shape[1]
    depth = ffn1_norm.shape[0]
    h = x.reshape(batch * seq, d)
    mem2 = mem.reshape(batch * n_mem, d)

    bf = lambda w: w.astype(BF16)
    ffn1_w_in, ffn1_w_out, ffn2_w_in, ffn2_w_out = map(bf, (ffn1_w_in, ffn1_w_out, ffn2_w_in, ffn2_w_out))
    mix_w_main, mix_w_gate = bf(mix_w_in[:, :, :4 * d]), bf(mix_w_in[:, :, 4 * d:])
    diff_w_out, sgu_w_out, conv_w_out, mix_w_o = map(bf, (diff_w_out, sgu_w_out, conv_w_out, mix_w_o))
    xattn_w_q, xattn_w_kv, xattn_w_o = map(bf, (xattn_w_q, xattn_w_kv, xattn_w_o))

    def ffn(h, norm, w_in, w_out):
        return matmul_residual(norm_swiglu_in(h, norm, w_in), w_out, h, 0.5)

    for l in range(depth):
        lam_init = 0.8 - 0.6 * math.exp(-0.3 * l)

        h = ffn(h, ffn1_norm[l], ffn1_w_in[l], ffn1_w_out[l])

        qkv, y_b, y_c = mixer_in(h, mix_norm[l], mix_w_main[l], sgu_norm[l], sgu_w_s[l], sgu_b[l],
                                 conv_w[l], seq=seq)
        y_a = diff_attention(qkv, diff_lambda[l], diff_subln[l], lam_init, batch=batch, seq=seq)
        merged = gated_merge(h, mix_norm[l], mix_w_gate[l], y_a, y_b, y_c,
                             diff_w_out[l], sgu_w_out[l], conv_w_out[l])
        h = matmul_residual(merged, mix_w_o[l], h, 1.0, bm_target=1024)

        kv = norm_matmul(mem2, mem_norm[l], xattn_w_kv[l]).reshape(batch, n_mem, -1)
        h = memory_attention(h, xattn_norm[l], xattn_w_q[l], kv, xattn_w_o[l], seq=seq)

        h = ffn(h, ffn2_norm[l], ffn2_w_in[l], ffn2_w_out[l])

    return output_norm(h, final_norm).reshape(batch, seq, d)
```

```python
import functools
import math

import jax
import jax.numpy as jnp
from jax import lax
from jax.experimental import pallas as pl
from jax.experimental.pallas import tpu as pltpu

HEAD_DIM = 128
SGU_CHUNK = 128
N_MEM_HEADS = 4
N_BRANCHES = 3
CONV_TAPS = 3
RMS_EPS = 1e-6
NEG_BIG = -1e30
V7X_VMEM_LIMIT_BYTES = 56 * 1024 * 1024
NORM_ROWS = 256

BF16 = jnp.bfloat16
F32 = jnp.float32


def _params(*sem):
    return pltpu.CompilerParams(dimension_semantics=sem, vmem_limit_bytes=V7X_VMEM_LIMIT_BYTES)


def _pick(total, target):
    b = min(total, target)
    while total % b:
        b //= 2
    return b


def _rms_rows(x, g):
    ms = jnp.mean(x * x, axis=-1, keepdims=True)
    return x * lax.rsqrt(ms + RMS_EPS) * g


def _fill_norm(x_ref, g_ref, n_ref):
    bm = x_ref.shape[0]
    rows = min(NORM_ROWS, bm)
    g = g_ref[...]

    def body(c, carry):
        r0 = pl.multiple_of(c * rows, rows)
        n_ref[pl.ds(r0, rows), :] = _rms_rows(x_ref[pl.ds(r0, rows), :], g).astype(n_ref.dtype)
        return carry

    lax.fori_loop(0, bm // rows, body, 0)


def _dot(a, b):
    return jnp.dot(a, b, preferred_element_type=F32)


def _dot_t(a, b):
    return lax.dot_general(a, b, (((1,), (1,)), ((), ())), preferred_element_type=F32)


def _gelu(x):
    return 0.5 * x * (1.0 + lax.erf(x * (1.0 / math.sqrt(2.0))))


def _norm_mm_kernel(x_ref, g_ref, w_ref, o_ref, n_ref):
    @pl.when(pl.program_id(1) == 0)
    def _():
        _fill_norm(x_ref, g_ref, n_ref)

    o_ref[...] = _dot(n_ref[...], w_ref[...]).astype(o_ref.dtype)


def norm_matmul(x, gain, w, *, bm_target=1024, bn_target=1024):
    m, d = x.shape
    n = w.shape[1]
    bm, bn = _pick(m, bm_target), _pick(n, bn_target)
    return pl.pallas_call(
        _norm_mm_kernel,
        grid=(m // bm, n // bn),
        in_specs=[pl.BlockSpec((bm, d), lambda i, j: (i, 0)),
                  pl.BlockSpec((1, d), lambda i, j: (0, 0)),
                  pl.BlockSpec((d, bn), lambda i, j: (0, j))],
        out_specs=pl.BlockSpec((bm, bn), lambda i, j: (i, j)),
        out_shape=jax.ShapeDtypeStruct((m, n), BF16),
        scratch_shapes=[pltpu.VMEM((bm, d), BF16)],
        compiler_params=_params("parallel", "arbitrary"),
        name="norm_matmul",
    )(x, gain.reshape(1, d), w)


def _norm_swiglu_kernel(x_ref, g_ref, wa_ref, wb_ref, o_ref, n_ref):
    @pl.when(pl.program_id(1) == 0)
    def _():
        _fill_norm(x_ref, g_ref, n_ref)

    n = n_ref[...]
    a = _dot(n, wa_ref[...])
    b = _dot(n, wb_ref[...])
    o_ref[...] = (a * jax.nn.sigmoid(a) * b).astype(o_ref.dtype)


def norm_swiglu_in(x, gain, w_in, *, bm_target=1024, bn_target=512):
    m, d = x.shape
    f = w_in.shape[1] // 2
    bm, bn = _pick(m, bm_target), _pick(f, bn_target)
    nb = f // bn
    return pl.pallas_call(
        _norm_swiglu_kernel,
        grid=(m // bm, nb),
        in_specs=[pl.BlockSpec((bm, d), lambda i, j: (i, 0)),
                  pl.BlockSpec((1, d), lambda i, j: (0, 0)),
                  pl.BlockSpec((d, bn), lambda i, j: (0, j)),
                  pl.BlockSpec((d, bn), lambda i, j: (0, j + nb))],
        out_specs=pl.BlockSpec((bm, bn), lambda i, j: (i, j)),
        out_shape=jax.ShapeDtypeStruct((m, f), BF16),
        scratch_shapes=[pltpu.VMEM((bm, d), BF16)],
        compiler_params=_params("parallel", "arbitrary"),
        name="norm_swiglu_in",
    )(x, gain.reshape(1, d), w_in, w_in)


def _mm_res_kernel(x_ref, w_ref, r_ref, o_ref, *, scale):
    o_ref[...] = r_ref[...] + scale * _dot(x_ref[...], w_ref[...])


def matmul_residual(x, w, res, scale, *, bm_target=512, bn_target=1024):
    m, k = x.shape
    n = w.shape[1]
    bm, bn = _pick(m, bm_target), _pick(n, bn_target)
    return pl.pallas_call(
        functools.partial(_mm_res_kernel, scale=scale),
        grid=(n // bn, m // bm),
        in_specs=[pl.BlockSpec((bm, k), lambda j, i: (i, 0)),
                  pl.BlockSpec((k, bn), lambda j, i: (0, j)),
                  pl.BlockSpec((bm, bn), lambda j, i: (i, j))],
        out_specs=pl.BlockSpec((bm, bn), lambda j, i: (i, j)),
        out_shape=jax.ShapeDtypeStruct((m, n), F32),
        compiler_params=_params("parallel", "parallel"),
        name="matmul_residual",
    )(x, w, res)


def _mix_in_kernel(x_ref, g_ref, w_ref, sgun_ref, ws_ref, sb_ref, cw_ref,
                   qkv_ref, yb_ref, yc_ref,
                   n_ref, u_ref, gv_ref, cb_ref, cc_ref, carry_ref, *, blocks_per_seq):
    i, j = pl.program_id(0), pl.program_id(1)
    bm, width = u_ref.shape
    n_groups = width // HEAD_DIM
    n_chunks = bm // SGU_CHUNK

    @pl.when(j == 0)
    def _():
        _fill_norm(x_ref, g_ref, n_ref)

    acc = _dot(n_ref[...], w_ref[...])

    @pl.when(j < 3)
    def _():
        qkv_ref[...] = acc.astype(qkv_ref.dtype)

    @pl.when(j == 3)
    def _():
        u_ref[...] = _gelu(acc)

    @pl.when(j == 4)
    def _():
        gv_ref[...] = _rms_rows(_gelu(acc), sgun_ref[...]).astype(gv_ref.dtype)
        t_idx = lax.broadcasted_iota(jnp.int32, (SGU_CHUNK, SGU_CHUNK), 0)
        s_idx = lax.broadcasted_iota(jnp.int32, (SGU_CHUNK, SGU_CHUNK), 1)
        for g in range(n_groups):
            cols = slice(g * HEAD_DIM, (g + 1) * HEAD_DIM)
            w_tril = jnp.where(s_idx <= t_idx, ws_ref[g], 0.0).astype(BF16)
            xg = jnp.concatenate(
                [gv_ref[c * SGU_CHUNK:(c + 1) * SGU_CHUNK, cols] for c in range(n_chunks)], axis=1)
            mixed = _dot(w_tril, xg) + sb_ref[g]
            for c in range(n_chunks):
                rows = slice(c * SGU_CHUNK, (c + 1) * SGU_CHUNK)
                yb_ref[rows, cols] = (
                    u_ref[rows, cols] * mixed[:, c * HEAD_DIM:(c + 1) * HEAD_DIM]).astype(yb_ref.dtype)

    @pl.when(j == 5)
    def _():
        cb_ref[...] = acc

    @pl.when(j == 6)
    def _():
        cc_ref[...] = acc

    @pl.when(j == 7)
    def _():
        @pl.when(i % blocks_per_seq == 0)
        def _():
            carry_ref[...] = jnp.zeros_like(carry_ref)

        z = cc_ref[...] * acc
        prev1 = carry_ref[7:8, :]
        prev2 = carry_ref[6:7, :]
        row = lax.broadcasted_iota(jnp.int32, z.shape, 0)
        z1 = jnp.where(row == 0, prev1, pltpu.roll(z, 1, 0))
        z2 = jnp.where(row == 0, prev2, jnp.where(row == 1, prev1, pltpu.roll(z, 2, 0)))
        conv = cw_ref[0:1, :] * z2 + cw_ref[1:2, :] * z1 + cw_ref[2:3, :] * z
        yc_ref[...] = (cb_ref[...] * conv).astype(yc_ref.dtype)
        carry_ref[...] = z[bm - 8:bm, :]


def mixer_in(x, gain, w, sgu_norm, sgu_w_s, sgu_b, conv_w, *, seq, bm_target=512):
    m, d = x.shape
    width = d // 2
    assert w.shape == (d, 8 * width)
    bm = _pick(seq, bm_target)
    assert bm % SGU_CHUNK == 0 and bm >= 8
    n_groups = width // HEAD_DIM
    const2 = lambda i, j: (0, 0)
    const3 = lambda i, j: (0, 0, 0)
    row_blk = lambda i, j: (i, 0)
    return pl.pallas_call(
        functools.partial(_mix_in_kernel, blocks_per_seq=seq // bm),
        grid=(m // bm, 8),
        in_specs=[pl.BlockSpec((bm, d), row_blk),
                  pl.BlockSpec((1, d), const2),
                  pl.BlockSpec((d, width), lambda i, j: (0, j)),
                  pl.BlockSpec((1, width), const2),
                  pl.BlockSpec((n_groups, SGU_CHUNK, SGU_CHUNK), const3),
                  pl.BlockSpec((n_groups, SGU_CHUNK, 1), const3),
                  pl.BlockSpec((CONV_TAPS, width), const2)],
        out_specs=[pl.BlockSpec((bm, width), lambda i, j: (i, jnp.minimum(j, 2))),
                   pl.BlockSpec((bm, width), row_blk),
                   pl.BlockSpec((bm, width), row_blk)],
        out_shape=[jax.ShapeDtypeStruct((m, 3 * width), BF16),
                   jax.ShapeDtypeStruct((m, width), BF16),
                   jax.ShapeDtypeStruct((m, width), BF16)],
        scratch_shapes=[pltpu.VMEM((bm, d), BF16),
                        pltpu.VMEM((bm, width), F32),
                        pltpu.VMEM((bm, width), BF16),
                        pltpu.VMEM((bm, width), F32),
                        pltpu.VMEM((bm, width), F32),
                        pltpu.VMEM((8, width), F32)],
        compiler_params=_params("arbitrary", "arbitrary"),
        name="mixer_in",
    )(x, gain.reshape(1, d), w, sgu_norm.reshape(1, width), sgu_w_s,
      sgu_b.reshape(n_groups, SGU_CHUNK, 1), conv_w)


def _diff_attn_kernel(scal_ref, q_ref, k_ref, v_ref, lam_ref, sub_ref, o_ref, kt_ref, *, n_heads, tq):
    h = pl.program_id(1)
    slope = scal_ref[h]
    lam_init = scal_ref[n_heads]
    half = HEAD_DIM // 2
    seq = q_ref.shape[1]

    kt_ref[...] = k_ref[0].T

    lp = lam_ref[...]
    lam = (jnp.exp(jnp.sum(lp[0:1] * lp[1:2], axis=-1, keepdims=True))
           - jnp.exp(jnp.sum(lp[2:3] * lp[3:4], axis=-1, keepdims=True)) + lam_init)
    sub = sub_ref[...]

    lane = lax.broadcasted_iota(jnp.int32, (tq, HEAD_DIM), 1)
    r_idx = lax.broadcasted_iota(jnp.int32, (2 * tq, tq), 0)
    c_idx = lax.broadcasted_iota(jnp.int32, (2 * tq, tq), 1)
    causal = c_idx <= jnp.where(r_idx >= tq, r_idx - tq, r_idx)

    for i in range(seq // tq):
        start, stop = i * tq, (i + 1) * tq
        q = q_ref[0, start:stop, :].astype(F32) * (1.0 / math.sqrt(half))
        qq = jnp.concatenate([jnp.where(lane < half, q, 0.0),
                              jnp.where(lane >= half, q, 0.0)], axis=0).astype(BF16)
        s = _dot(qq, kt_ref[:, :stop])
        k_rel = lax.broadcasted_iota(jnp.int32, (1, stop), 1) - start
        t = s + slope * k_rel.astype(F32)
        t_diag = jnp.where(causal, t[:, start:], NEG_BIG)
        m = jnp.max(t_diag, axis=-1, keepdims=True)
        if i:
            t_past = t[:, :start]
            m = jnp.maximum(m, jnp.max(t_past, axis=-1, keepdims=True))
            p = jnp.concatenate([jnp.exp(t_past - m), jnp.exp(t_diag - m)], axis=1)
        else:
            p = jnp.exp(t_diag - m)
        denom = jnp.sum(p, axis=-1, keepdims=True)
        a = _dot(p.astype(BF16), v_ref[0, :stop, :]) / denom
        o = a[:tq] - lam * a[tq:]
        o_ref[0, start:stop, :] = (_rms_rows(o, sub) * (1.0 - lam_init)).astype(o_ref.dtype)


def diff_attention(qkv, lam_params, subln, lam_init, *, batch, seq, tq_target=256):
    width = qkv.shape[1] // 3
    n_heads = width // HEAD_DIM
    tq = _pick(seq, tq_target)
    slopes = 2.0 ** (-8.0 * jnp.arange(1, n_heads + 1, dtype=F32) / n_heads)
    scal = jnp.concatenate([slopes, jnp.full((1,), lam_init, F32)])
    qkv3 = qkv.reshape(batch, seq, 3 * width)
    head_blk = lambda off: pl.BlockSpec((1, seq, HEAD_DIM), lambda b, h: (b, 0, off + h))
    out = pl.pallas_call(
        functools.partial(_diff_attn_kernel, n_heads=n_heads, tq=tq),
        grid=(batch, n_heads),
        in_specs=[pl.BlockSpec(memory_space=pltpu.SMEM),
                  head_blk(0), head_blk(n_heads), head_blk(2 * n_heads),
                  pl.BlockSpec(lam_params.shape, lambda b, h: (0, 0)),
                  pl.BlockSpec((1, HEAD_DIM), lambda b, h: (0, 0))],
        out_specs=head_blk(0),
        out_shape=jax.ShapeDtypeStruct((batch, seq, width), BF16),
        scratch_shapes=[pltpu.VMEM((HEAD_DIM, seq), BF16)],
        compiler_params=_params("parallel", "parallel"),
        name="diff_attention",
    )(scal, qkv3, qkv3, qkv3, lam_params, subln.reshape(1, HEAD_DIM))
    return out.reshape(batch * seq, width)


def _merge_kernel(x_ref, g_ref, wg0_ref, wg1_ref, wg2_ref, ya_ref, yb_ref, yc_ref,
                  wa_ref, wb_ref, wc_ref, o_ref, n_ref):
    @pl.when(pl.program_id(1) == 0)
    def _():
        _fill_norm(x_ref, g_ref, n_ref)

    n = n_ref[...]

    def branch(wg_ref, y_ref, w_ref):
        return jax.nn.sigmoid(_dot(n, wg_ref[...])) * _dot(y_ref[...], w_ref[...])

    merged = branch(wg0_ref, ya_ref, wa_ref) + branch(wg1_ref, yb_ref, wb_ref)
    o_ref[...] = (merged + branch(wg2_ref, yc_ref, wc_ref)).astype(o_ref.dtype)


def gated_merge(x, gain, w_gate, y_a, y_b, y_c, w_a, w_b, w_c, *, bm_target=512, bn_target=512):
    m, d = x.shape
    width = y_a.shape[1]
    bm, bn = _pick(m, bm_target), _pick(d, bn_target)
    nb = d // bn
    row_blk = lambda i, j: (i, 0)
    col_blk = lambda i, j: (0, j)
    return pl.pallas_call(
        _merge_kernel,
        grid=(m // bm, nb),
        in_specs=[pl.BlockSpec((bm, d), row_blk),
                  pl.BlockSpec((1, d), lambda i, j: (0, 0)),
                  pl.BlockSpec((d, bn), col_blk),
                  pl.BlockSpec((d, bn), lambda i, j: (0, j + nb)),
                  pl.BlockSpec((d, bn), lambda i, j: (0, j + 2 * nb)),
                  pl.BlockSpec((bm, width), row_blk),
                  pl.BlockSpec((bm, width), row_blk),
                  pl.BlockSpec((bm, width), row_blk),
                  pl.BlockSpec((width, bn), col_blk),
                  pl.BlockSpec((width, bn), col_blk),
                  pl.BlockSpec((width, bn), col_blk)],
        out_specs=pl.BlockSpec((bm, bn), lambda i, j: (i, j)),
        out_shape=jax.ShapeDtypeStruct((m, d), BF16),
        scratch_shapes=[pltpu.VMEM((bm, d), BF16)],
        compiler_params=_params("parallel", "arbitrary"),
        name="gated_merge",
    )(x, gain.reshape(1, d), w_gate, w_gate, w_gate, y_a, y_b, y_c, w_a, w_b, w_c)


def _mem_attn_kernel(x_ref, g_ref, wq_ref, kv_ref, wo_ref, o_ref, n_ref):
    mem_width = wq_ref.shape[1]
    _fill_norm(x_ref, g_ref, n_ref)
    q = _dot(n_ref[...], wq_ref[...]).astype(BF16)
    kv = kv_ref[0]
    scale = 1.0 / math.sqrt(HEAD_DIM)
    heads = []
    for hd in range(mem_width // HEAD_DIM):
        cols = slice(hd * HEAD_DIM, (hd + 1) * HEAD_DIM)
        k_h = kv[:, hd * HEAD_DIM:(hd + 1) * HEAD_DIM]
        v_h = kv[:, mem_width + hd * HEAD_DIM:mem_width + (hd + 1) * HEAD_DIM]
        s = _dot_t(q[:, cols], k_h) * scale
        p = jnp.exp(s - jnp.max(s, axis=-1, keepdims=True))
        denom = jnp.sum(p, axis=-1, keepdims=True)
        heads.append(_dot(p.astype(BF16), v_h) / denom)
    o = jnp.concatenate(heads, axis=1).astype(BF16)
    o_ref[...] = x_ref[...] + _dot(o, wo_ref[...])


def memory_attention(x, gain, w_q, kv, w_o, *, seq, bm_target=512):
    m, d = x.shape
    mem_width = w_q.shape[1]
    bm = _pick(seq, bm_target)
    per_seq = seq // bm
    return pl.pallas_call(
        _mem_attn_kernel,
        grid=(m // bm,),
        in_specs=[pl.BlockSpec((bm, d), lambda i: (i, 0)),
                  pl.BlockSpec((1, d), lambda i: (0, 0)),
                  pl.BlockSpec((d, mem_width), lambda i: (0, 0)),
                  pl.BlockSpec((1,) + kv.shape[1:], lambda i: (i // per_seq, 0, 0)),
                  pl.BlockSpec((mem_width, d), lambda i: (0, 0))],
        out_specs=pl.BlockSpec((bm, d), lambda i: (i, 0)),
        out_shape=jax.ShapeDtypeStruct((m, d), F32),
        scratch_shapes=[pltpu.VMEM((bm, d), BF16)],
        compiler_params=_params("parallel"),
        name="memory_attention",
    )(x, gain.reshape(1, d), w_q, kv, w_o)


def _final_norm_kernel(x_ref, g_ref, o_ref):
    o_ref[...] = _rms_rows(x_ref[...], g_ref[...])


def output_norm(x, gain, *, bm_target=256):
    m, d = x.shape
    bm = _pick(m, bm_target)
    return pl.pallas_call(
        _final_norm_kernel,
        grid=(m // bm,),
        in_specs=[pl.BlockSpec((bm, d), lambda i: (i, 0)),
                  pl.BlockSpec((1, d), lambda i: (0, 0))],
        out_specs=pl.BlockSpec((bm, d), lambda i: (i, 0)),
        out_shape=jax.ShapeDtypeStruct((m, d), F32),
        compiler_params=_params("parallel"),
        name="final_norm",
    )(x, gain.reshape(1, d))


def kernel(x, mem, ffn1_norm, ffn1_w_in, ffn1_w_out, mix_norm, mix_w_in, diff_lambda, diff_subln, diff_w_out, sgu_norm, sgu_w_s, sgu_b, sgu_w_out, conv_w, conv_w_out, mix_w_o, xattn_norm, mem_norm, xattn_w_q, xattn_w_kv, xattn_w_o, ffn2_norm, ffn2_w_in, ffn2_w_out, final_norm):
    batch, seq, d = x.shape
    n_mem = mem.shape[1]
    depth = ffn1_norm.shape[0]
    h = x.reshape(batch * seq, d)
    mem2 = mem.reshape(batch * n_mem, d)

    bf = lambda w: w.astype(BF16)
    ffn1_w_in, ffn1_w_out, ffn2_w_in, ffn2_w_out = map(bf, (ffn1_w_in, ffn1_w_out, ffn2_w_in, ffn2_w_out))
    mix_w_main, mix_w_gate = bf(mix_w_in[:, :, :4 * d]), bf(mix_w_in[:, :, 4 * d:])
    diff_w_out, sgu_w_out, conv_w_out, mix_w_o = map(bf, (diff_w_out, sgu_w_out, conv_w_out, mix_w_o))
    xattn_w_q, xattn_w_kv, xattn_w_o = map(bf, (xattn_w_q, xattn_w_kv, xattn_w_o))

    def ffn(h, norm, w_in, w_out):
        return matmul_residual(norm_swiglu_in(h, norm, w_in), w_out, h, 0.5)

    for l in range(depth):
        lam_init = 0.8 - 0.6 * math.exp(-0.3 * l)

        h = ffn(h, ffn1_norm[l], ffn1_w_in[l], ffn1_w_out[l])

        qkv, y_b, y_c = mixer_in(h, mix_norm[l], mix_w_main[l], sgu_norm[l], sgu_w_s[l], sgu_b[l],
                                 conv_w[l], seq=seq)
        y_a = diff_attention(qkv, diff_lambda[l], diff_subln[l], lam_init, batch=batch, seq=seq)
        merged = gated_merge(h, mix_norm[l], mix_w_gate[l], y_a, y_b, y_c,
                             diff_w_out[l], sgu_w_out[l], conv_w_out[l])
        h = matmul_residual(merged, mix_w_o[l], h, 1.0, bm_target=1024)

        kv = norm_matmul(mem2, mem_norm[l], xattn_w_kv[l]).reshape(batch, n_mem, -1)
        h = memory_attention(h, xattn_norm[l], xattn_w_q[l], kv, xattn_w_o[l], seq=seq)

        h = ffn(h, ffn2_norm[l], ffn2_w_in[l], ffn2_w_out[l])

    return output_norm(h, final_norm).reshape(batch, seq, d)
```

```python
import functools
import math

import jax
import jax.numpy as jnp
from jax import lax
from jax.experimental import pallas as pl
from jax.experimental.pallas import tpu as pltpu

HEAD_DIM = 128
SGU_CHUNK = 128
N_MEM_HEADS = 4
N_BRANCHES = 3
CONV_TAPS = 3
RMS_EPS = 1e-6
NEG_BIG = -1e30
V7X_VMEM_LIMIT_BYTES = 56 * 1024 * 1024
NORM_ROWS = 256

BF16 = jnp.bfloat16
F32 = jnp.float32


def _params(*sem):
    return pltpu.CompilerParams(dimension_semantics=sem, vmem_limit_bytes=V7X_VMEM_LIMIT_BYTES)


def _pick(total, target):
    b = min(total, target)
    while total % b:
        b //= 2
    return b


def _rms_rows(x, g):
    ms = jnp.mean(x * x, axis=-1, keepdims=True)
    return x * lax.rsqrt(ms + RMS_EPS) * g


def _fill_norm(x_ref, g_ref, n_ref):
    bm = x_ref.shape[0]
    rows = min(NORM_ROWS, bm)
    g = g_ref[...]

    def body(c, carry):
        r0 = pl.multiple_of(c * rows, rows)
        n_ref[pl.ds(r0, rows), :] = _rms_rows(x_ref[pl.ds(r0, rows), :], g).astype(n_ref.dtype)
        return carry

    lax.fori_loop(0, bm // rows, body, 0)


def _dot(a, b):
    return jnp.dot(a, b, preferred_element_type=F32)


def _dot_t(a, b):
    return lax.dot_general(a, b, (((1,), (1,)), ((), ())), preferred_element_type=F32)


def _gelu(x):
    return 0.5 * x * (1.0 + lax.erf(x * (1.0 / math.sqrt(2.0))))


def _norm_mm_kernel(x_ref, g_ref, w_ref, o_ref, n_ref):
    @pl.when(pl.program_id(1) == 0)
    def _():
        _fill_norm(x_ref, g_ref, n_ref)

    o_ref[...] = _dot(n_ref[...], w_ref[...]).astype(o_ref.dtype)


def _layer_spec(block, layer, index_map):
    return pl.BlockSpec((None,) + block, lambda *g: (layer,) + index_map(*g))


def norm_matmul(x, gain, w, layer, *, bm_target=1024, bn_target=1024):
    m, d = x.shape
    n = w.shape[2]
    bm, bn = _pick(m, bm_target), _pick(n, bn_target)
    return pl.pallas_call(
        _norm_mm_kernel,
        grid=(m // bm, n // bn),
        in_specs=[pl.BlockSpec((bm, d), lambda i, j: (i, 0)),
                  pl.BlockSpec((1, d), lambda i, j: (0, 0)),
                  _layer_spec((d, bn), layer, lambda i, j: (0, j))],
        out_specs=pl.BlockSpec((bm, bn), lambda i, j: (i, j)),
        out_shape=jax.ShapeDtypeStruct((m, n), BF16),
        scratch_shapes=[pltpu.VMEM((bm, d), BF16)],
        compiler_params=_params("parallel", "arbitrary"),
        name="norm_matmul",
    )(x, gain.reshape(1, d), w)


def _norm_swiglu_kernel(x_ref, g_ref, wa_ref, wb_ref, o_ref, n_ref):
    @pl.when(pl.program_id(1) == 0)
    def _():
        _fill_norm(x_ref, g_ref, n_ref)

    n = n_ref[...]
    a = _dot(n, wa_ref[...])
    b = _dot(n, wb_ref[...])
    o_ref[...] = (a * jax.nn.sigmoid(a) * b).astype(o_ref.dtype)


def norm_swiglu_in(x, gain, w_in, layer, *, bm_target=1024, bn_target=512):
    m, d = x.shape
    f = w_in.shape[2] // 2
    bm, bn = _pick(m, bm_target), _pick(f, bn_target)
    nb = f // bn
    return pl.pallas_call(
        _norm_swiglu_kernel,
        grid=(m // bm, nb),
        in_specs=[pl.BlockSpec((bm, d), lambda i, j: (i, 0)),
                  pl.BlockSpec((1, d), lambda i, j: (0, 0)),
                  _layer_spec((d, bn), layer, lambda i, j: (0, j)),
                  _layer_spec((d, bn), layer, lambda i, j: (0, j + nb))],
        out_specs=pl.BlockSpec((bm, bn), lambda i, j: (i, j)),
        out_shape=jax.ShapeDtypeStruct((m, f), BF16),
        scratch_shapes=[pltpu.VMEM((bm, d), BF16)],
        compiler_params=_params("parallel", "arbitrary"),
        name="norm_swiglu_in",
    )(x, gain.reshape(1, d), w_in, w_in)


def _mm_res_kernel(x_ref, w_ref, r_ref, o_ref, *, scale):
    o_ref[...] = r_ref[...] + scale * _dot(x_ref[...], w_ref[...])


def matmul_residual(x, w, layer, res, scale, *, bm_target=512, bn_target=1024):
    m, k = x.shape
    n = w.shape[2]
    bm, bn = _pick(m, bm_target), _pick(n, bn_target)
    return pl.pallas_call(
        functools.partial(_mm_res_kernel, scale=scale),
        grid=(n // bn, m // bm),
        in_specs=[pl.BlockSpec((bm, k), lambda j, i: (i, 0)),
                  _layer_spec((k, bn), layer, lambda j, i: (0, j)),
                  pl.BlockSpec((bm, bn), lambda j, i: (i, j))],
        out_specs=pl.BlockSpec((bm, bn), lambda j, i: (i, j)),
        out_shape=jax.ShapeDtypeStruct((m, n), F32),
        compiler_params=_params("parallel", "parallel"),
        name="matmul_residual",
    )(x, w, res)


def _mix_in_kernel(x_ref, g_ref, w_ref, sgun_ref, ws_ref, sb_ref, cw_ref,
                   qkv_ref, yb_ref, yc_ref,
                   n_ref, u_ref, gv_ref, cb_ref, cc_ref, carry_ref, *, blocks_per_seq):
    i, j = pl.program_id(0), pl.program_id(1)
    bm, width = u_ref.shape
    n_groups = width // HEAD_DIM
    n_chunks = bm // SGU_CHUNK

    @pl.when(j == 0)
    def _():
        _fill_norm(x_ref, g_ref, n_ref)

    acc = _dot(n_ref[...], w_ref[...])

    @pl.when(j < 3)
    def _():
        qkv_ref[...] = acc.astype(qkv_ref.dtype)

    @pl.when(j == 3)
    def _():
        u_ref[...] = _gelu(acc)

    @pl.when(j == 4)
    def _():
        gv_ref[...] = _rms_rows(_gelu(acc), sgun_ref[...]).astype(gv_ref.dtype)
        t_idx = lax.broadcasted_iota(jnp.int32, (SGU_CHUNK, SGU_CHUNK), 0)
        s_idx = lax.broadcasted_iota(jnp.int32, (SGU_CHUNK, SGU_CHUNK), 1)
        for g in range(n_groups):
            cols = slice(g * HEAD_DIM, (g + 1) * HEAD_DIM)
            w_tril = jnp.where(s_idx <= t_idx, ws_ref[g], 0.0).astype(BF16)
            xg = jnp.concatenate(
                [gv_ref[c * SGU_CHUNK:(c + 1) * SGU_CHUNK, cols] for c in range(n_chunks)], axis=1)
            mixed = _dot(w_tril, xg) + sb_ref[g]
            for c in range(n_chunks):
                rows = slice(c * SGU_CHUNK, (c + 1) * SGU_CHUNK)
                yb_ref[rows, cols] = (
                    u_ref[rows, cols] * mixed[:, c * HEAD_DIM:(c + 1) * HEAD_DIM]).astype(yb_ref.dtype)

    @pl.when(j == 5)
    def _():
        cb_ref[...] = acc

    @pl.when(j == 6)
    def _():
        cc_ref[...] = acc

    @pl.when(j == 7)
    def _():
        @pl.when(i % blocks_per_seq == 0)
        def _():
            carry_ref[...] = jnp.zeros_like(carry_ref)

        z = cc_ref[...] * acc
        prev1 = carry_ref[7:8, :]
        prev2 = carry_ref[6:7, :]
        row = lax.broadcasted_iota(jnp.int32, z.shape, 0)
        z1 = jnp.where(row == 0, prev1, pltpu.roll(z, 1, 0))
        z2 = jnp.where(row == 0, prev2, jnp.where(row == 1, prev1, pltpu.roll(z, 2, 0)))
        conv = cw_ref[0:1, :] * z2 + cw_ref[1:2, :] * z1 + cw_ref[2:3, :] * z
        yc_ref[...] = (cb_ref[...] * conv).astype(yc_ref.dtype)
        carry_ref[...] = z[bm - 8:bm, :]


def mixer_in(x, gain, w, layer, sgu_norm, sgu_w_s, sgu_b, conv_w, *, seq, bm_target=512):
    m, d = x.shape
    width = d // 2
    assert w.shape[1] == d and w.shape[2] >= 8 * width
    bm = _pick(seq, bm_target)
    assert bm % SGU_CHUNK == 0 and bm >= 8
    n_groups = width // HEAD_DIM
    const2 = lambda i, j: (0, 0)
    const3 = lambda i, j: (0, 0, 0)
    row_blk = lambda i, j: (i, 0)
    return pl.pallas_call(
        functools.partial(_mix_in_kernel, blocks_per_seq=seq // bm),
        grid=(m // bm, 8),
        in_specs=[pl.BlockSpec((bm, d), row_blk),
                  pl.BlockSpec((1, d), const2),
                  _layer_spec((d, width), layer, lambda i, j: (0, j)),
                  pl.BlockSpec((1, width), const2),
                  pl.BlockSpec((n_groups, SGU_CHUNK, SGU_CHUNK), const3),
                  pl.BlockSpec((n_groups, SGU_CHUNK, 1), const3),
                  pl.BlockSpec((CONV_TAPS, width), const2)],
        out_specs=[pl.BlockSpec((bm, width), lambda i, j: (i, jnp.minimum(j, 2))),
                   pl.BlockSpec((bm, width), row_blk),
                   pl.BlockSpec((bm, width), row_blk)],
        out_shape=[jax.ShapeDtypeStruct((m, 3 * width), BF16),
                   jax.ShapeDtypeStruct((m, width), BF16),
                   jax.ShapeDtypeStruct((m, width), BF16)],
        scratch_shapes=[pltpu.VMEM((bm, d), BF16),
                        pltpu.VMEM((bm, width), F32),
                        pltpu.VMEM((bm, width), BF16),
                        pltpu.VMEM((bm, width), F32),
                        pltpu.VMEM((bm, width), F32),
                        pltpu.VMEM((8, width), F32)],
        compiler_params=_params("arbitrary", "arbitrary"),
        name="mixer_in",
    )(x, gain.reshape(1, d), w, sgu_norm.reshape(1, width), sgu_w_s,
      sgu_b.reshape(n_groups, SGU_CHUNK, 1), conv_w)


def _bf16_split3(x):
    p1 = x.astype(BF16).astype(F32)
    r1 = x - p1
    p2 = r1.astype(BF16).astype(F32)
    p3 = (r1 - p2).astype(BF16).astype(F32)
    return p1, p2, p3


def _diff_attn_kernel(scal_ref, q_ref, k_ref, v_ref, lam_ref, sub_ref, o_ref, kt_ref, vx_ref,
                      *, n_heads, tq):
    h = pl.program_id(1)
    slope = scal_ref[h]
    lam_init = scal_ref[n_heads]
    half = HEAD_DIM // 2
    seq = q_ref.shape[1]

    pos = lax.broadcasted_iota(jnp.int32, (HEAD_DIM, seq), 1)
    frow = lax.broadcasted_iota(jnp.int32, (HEAD_DIM, seq), 0)
    pos_hi = (pos - pos % HEAD_DIM).astype(F32)
    pos_lo = (pos % HEAD_DIM).astype(F32)
    kt_ref[:HEAD_DIM, :] = k_ref[0].T
    kt_ref[HEAD_DIM:, :] = jnp.where(frow < 3, pos_hi, jnp.where(frow < 6, pos_lo, 0.0)).astype(BF16)
    vx_ref[:, :HEAD_DIM] = v_ref[0]
    vx_ref[:, HEAD_DIM:] = jnp.ones((seq, HEAD_DIM), BF16)

    s1, s2, s3 = _bf16_split3(jnp.full((2 * tq, HEAD_DIM), slope, F32))
    flane = lax.broadcasted_iota(jnp.int32, (2 * tq, HEAD_DIM), 1)
    piece = jnp.where(flane % 3 == 0, s1, jnp.where(flane % 3 == 1, s2, s3))
    q_feat = jnp.where(flane < 6, piece, 0.0).astype(BF16)

    lp = lam_ref[...]
    lam = (jnp.exp(jnp.sum(lp[0:1] * lp[1:2], axis=-1, keepdims=True))
           - jnp.exp(jnp.sum(lp[2:3] * lp[3:4], axis=-1, keepdims=True)) + lam_init)
    sub = sub_ref[...]

    lane = lax.broadcasted_iota(jnp.int32, (tq, HEAD_DIM), 1)
    r_idx = lax.broadcasted_iota(jnp.int32, (2 * tq, tq), 0)
    c_idx = lax.broadcasted_iota(jnp.int32, (2 * tq, tq), 1)
    causal = c_idx <= jnp.where(r_idx >= tq, r_idx - tq, r_idx)

    for i in reversed(range(seq // tq)):
        start, stop = i * tq, (i + 1) * tq
        q = q_ref[0, start:stop, :].astype(F32) * (1.0 / math.sqrt(half))
        qq = jnp.concatenate([jnp.where(lane < half, q, 0.0),
                              jnp.where(lane >= half, q, 0.0)], axis=0).astype(BF16)
        t = _dot(jnp.concatenate([qq, q_feat], axis=1), kt_ref[:, :stop])
        t_diag = jnp.where(causal, t[:, start:], NEG_BIG)
        m = jnp.max(t_diag, axis=-1, keepdims=True)
        if i:
            t_past = t[:, :start]
            m = jnp.maximum(m, jnp.max(t_past, axis=-1, keepdims=True))
            p = jnp.concatenate([jnp.exp(t_past - m), jnp.exp(t_diag - m)], axis=1)
        else:
            p = jnp.exp(t_diag - m)
        a = _dot(p.astype(BF16), vx_ref[:stop, :])
        a = a[:, :HEAD_DIM] / a[:, HEAD_DIM:]
        o = a[:tq] - lam * a[tq:]
        o_ref[0, start:stop, :] = (_rms_rows(o, sub) * (1.0 - lam_init)).astype(o_ref.dtype)


def diff_attention(qkv, lam_params, subln, lam_init, *, batch, seq, tq_target=256):
    width = qkv.shape[1] // 3
    n_heads = width // HEAD_DIM
    tq = _pick(seq, tq_target)
    slopes = 2.0 ** (-8.0 * jnp.arange(1, n_heads + 1, dtype=F32) / n_heads)
    scal = jnp.concatenate([slopes, jnp.full((1,), lam_init, F32)])
    qkv3 = qkv.reshape(batch, seq, 3 * width)
    head_blk = lambda off: pl.BlockSpec((1, seq, HEAD_DIM), lambda b, h: (b, 0, off + h))
    out = pl.pallas_call(
        functools.partial(_diff_attn_kernel, n_heads=n_heads, tq=tq),
        grid=(batch, n_heads),
        in_specs=[pl.BlockSpec(memory_space=pltpu.SMEM),
                  head_blk(0), head_blk(n_heads), head_blk(2 * n_heads),
                  pl.BlockSpec(lam_params.shape, lambda b, h: (0, 0)),
                  pl.BlockSpec((1, HEAD_DIM), lambda b, h: (0, 0))],
        out_specs=head_blk(0),
        out_shape=jax.ShapeDtypeStruct((batch, seq, width), BF16),
        scratch_shapes=[pltpu.VMEM((2 * HEAD_DIM, seq), BF16),
                        pltpu.VMEM((seq, 2 * HEAD_DIM), BF16)],
        compiler_params=_params("parallel", "parallel"),
        name="diff_attention",
    )(scal, qkv3, qkv3, qkv3, lam_params, subln.reshape(1, HEAD_DIM))
    return out.reshape(batch * seq, width)


def _merge_kernel(x_ref, g_ref, wg0_ref, wg1_ref, wg2_ref, ya_ref, yb_ref, yc_ref,
                  wa_ref, wb_ref, wc_ref, o_ref, n_ref):
    @pl.when(pl.program_id(1) == 0)
    def _():
        _fill_norm(x_ref, g_ref, n_ref)

    n = n_ref[...]

    def branch(wg_ref, y_ref, w_ref):
        return jax.nn.sigmoid(_dot(n, wg_ref[...])) * _dot(y_ref[...], w_ref[...])

    merged = branch(wg0_ref, ya_ref, wa_ref) + branch(wg1_ref, yb_ref, wb_ref)
    o_ref[...] = (merged + branch(wg2_ref, yc_ref, wc_ref)).astype(o_ref.dtype)


def gated_merge(x, gain, w_mix_in, layer, y_a, y_b, y_c, w_a, w_b, w_c, *, bm_target=512, bn_target=512):
    m, d = x.shape
    width = y_a.shape[1]
    bm, bn = _pick(m, bm_target), _pick(d, bn_target)
    nb = d // bn
    gate0 = (w_mix_in.shape[2] - N_BRANCHES * d) // bn
    row_blk = lambda i, j: (i, 0)
    col_blk = lambda i, j: (0, j)
    return pl.pallas_call(
        _merge_kernel,
        grid=(m // bm, nb),
        in_specs=[pl.BlockSpec((bm, d), row_blk),
                  pl.BlockSpec((1, d), lambda i, j: (0, 0)),
                  _layer_spec((d, bn), layer, lambda i, j: (0, gate0 + j)),
                  _layer_spec((d, bn), layer, lambda i, j: (0, gate0 + nb + j)),
                  _layer_spec((d, bn), layer, lambda i, j: (0, gate0 + 2 * nb + j)),
                  pl.BlockSpec((bm, width), row_blk),
                  pl.BlockSpec((bm, width), row_blk),
                  pl.BlockSpec((bm, width), row_blk),
                  _layer_spec((width, bn), layer, col_blk),
                  _layer_spec((width, bn), layer, col_blk),
                  _layer_spec((width, bn), layer, col_blk)],
        out_specs=pl.BlockSpec((bm, bn), lambda i, j: (i, j)),
        out_shape=jax.ShapeDtypeStruct((m, d), BF16),
        scratch_shapes=[pltpu.VMEM((bm, d), BF16)],
        compiler_params=_params("parallel", "arbitrary"),
        name="gated_merge",
    )(x, gain.reshape(1, d), w_mix_in, w_mix_in, w_mix_in, y_a, y_b, y_c, w_a, w_b, w_c)


def _mem_attn_kernel(x_ref, g_ref, wq_ref, kv_ref, wo_ref, o_ref, n_ref):
    mem_width = wq_ref.shape[1]
    _fill_norm(x_ref, g_ref, n_ref)
    q = _dot(n_ref[...], wq_ref[...]).astype(BF16)
    kv = kv_ref[0]
    scale = 1.0 / math.sqrt(HEAD_DIM)
    heads = []
    for hd in range(mem_width // HEAD_DIM):
        cols = slice(hd * HEAD_DIM, (hd + 1) * HEAD_DIM)
        k_h = kv[:, hd * HEAD_DIM:(hd + 1) * HEAD_DIM]
        v_h = kv[:, mem_width + hd * HEAD_DIM:mem_width + (hd + 1) * HEAD_DIM]
        s = _dot_t(q[:, cols], k_h) * scale
        p = jnp.exp(s - jnp.max(s, axis=-1, keepdims=True))
        denom = jnp.sum(p, axis=-1, keepdims=True)
        heads.append(_dot(p.astype(BF16), v_h) / denom)
    o = jnp.concatenate(heads, axis=1).astype(BF16)
    o_ref[...] = x_ref[...] + _dot(o, wo_ref[...])


def memory_attention(x, gain, w_q, kv, w_o, layer, *, seq, bm_target=512):
    m, d = x.shape
    mem_width = w_q.shape[2]
    bm = _pick(seq, bm_target)
    per_seq = seq // bm
    return pl.pallas_call(
        _mem_attn_kernel,
        grid=(m // bm,),
        in_specs=[pl.BlockSpec((bm, d), lambda i: (i, 0)),
                  pl.BlockSpec((1, d), lambda i: (0, 0)),
                  _layer_spec((d, mem_width), layer, lambda i: (0, 0)),
                  pl.BlockSpec((1,) + kv.shape[1:], lambda i: (i // per_seq, 0, 0)),
                  _layer_spec((mem_width, d), layer, lambda i: (0, 0))],
        out_specs=pl.BlockSpec((bm, d), lambda i: (i, 0)),
        out_shape=jax.ShapeDtypeStruct((m, d), F32),
        scratch_shapes=[pltpu.VMEM((bm, d), BF16)],
        compiler_params=_params("parallel"),
        name="memory_attention",
    )(x, gain.reshape(1, d), w_q, kv, w_o)


def _final_norm_kernel(x_ref, g_ref, o_ref):
    o_ref[...] = _rms_rows(x_ref[...], g_ref[...])


def output_norm(x, gain, *, bm_target=256):
    m, d = x.shape
    bm = _pick(m, bm_target)
    return pl.pallas_call(
        _final_norm_kernel,
        grid=(m // bm,),
        in_specs=[pl.BlockSpec((bm, d), lambda i: (i, 0)),
                  pl.BlockSpec((1, d), lambda i: (0, 0))],
        out_specs=pl.BlockSpec((bm, d), lambda i: (i, 0)),
        out_shape=jax.ShapeDtypeStruct((m, d), F32),
        compiler_params=_params("parallel"),
        name="final_norm",
    )(x, gain.reshape(1, d))


def kernel(x, mem, ffn1_norm, ffn1_w_in, ffn1_w_out, mix_norm, mix_w_in, diff_lambda, diff_subln, diff_w_out, sgu_norm, sgu_w_s, sgu_b, sgu_w_out, conv_w, conv_w_out, mix_w_o, xattn_norm, mem_norm, xattn_w_q, xattn_w_kv, xattn_w_o, ffn2_norm, ffn2_w_in, ffn2_w_out, final_norm):
    batch, seq, d = x.shape
    n_mem = mem.shape[1]
    depth = ffn1_norm.shape[0]
    h = x.reshape(batch * seq, d)
    mem2 = mem.reshape(batch * n_mem, d)

    bf = lambda w: w.astype(BF16)
    ffn1_w_in, ffn1_w_out, ffn2_w_in, ffn2_w_out = map(bf, (ffn1_w_in, ffn1_w_out, ffn2_w_in, ffn2_w_out))
    mix_w_in, diff_w_out, sgu_w_out, conv_w_out, mix_w_o = map(
        bf, (mix_w_in, diff_w_out, sgu_w_out, conv_w_out, mix_w_o))
    xattn_w_q, xattn_w_kv, xattn_w_o = map(bf, (xattn_w_q, xattn_w_kv, xattn_w_o))

    def ffn(h, norm, w_in, w_out, l):
        return matmul_residual(norm_swiglu_in(h, norm, w_in, l), w_out, l, h, 0.5)

    for l in range(depth):
        lam_init = 0.8 - 0.6 * math.exp(-0.3 * l)

        h = ffn(h, ffn1_norm[l], ffn1_w_in, ffn1_w_out, l)

        qkv, y_b, y_c = mixer_in(h, mix_norm[l], mix_w_in, l, sgu_norm[l], sgu_w_s[l], sgu_b[l],
                                 conv_w[l], seq=seq)
        y_a = diff_attention(qkv, diff_lambda[l], diff_subln[l], lam_init, batch=batch, seq=seq)
        merged = gated_merge(h, mix_norm[l], mix_w_in, l, y_a, y_b, y_c,
                             diff_w_out, sgu_w_out, conv_w_out)
        h = matmul_residual(merged, mix_w_o, l, h, 1.0, bm_target=1024)

        kv = norm_matmul(mem2, mem_norm[l], xattn_w_kv, l).reshape(batch, n_mem, -1)
        h = memory_attention(h, xattn_norm[l], xattn_w_q, kv, xattn_w_o, l, seq=seq)

        h = ffn(h, ffn2_norm[l], ffn2_w_in, ffn2_w_out, l)

    return output_norm(h, final_norm).reshape(batch, seq, d)
```

```python
import functools
import math

import jax
import jax.numpy as jnp
from jax import lax
from jax.experimental import pallas as pl
from jax.experimental.pallas import tpu as pltpu

HEAD_DIM = 128
SGU_CHUNK = 128
N_MEM_HEADS = 4
N_BRANCHES = 3
CONV_TAPS = 3
RMS_EPS = 1e-6
NEG_BIG = -1e30
V7X_VMEM_LIMIT_BYTES = 56 * 1024 * 1024
NORM_ROWS = 256

BF16 = jnp.bfloat16
F32 = jnp.float32


def _params(*sem):
    return pltpu.CompilerParams(dimension_semantics=sem, vmem_limit_bytes=V7X_VMEM_LIMIT_BYTES)


def _pick(total, target):
    b = min(total, target)
    while total % b:
        b //= 2
    return b


def _rms_rows(x, g):
    ms = jnp.mean(x * x, axis=-1, keepdims=True)
    return x * lax.rsqrt(ms + RMS_EPS) * g


def _fill_norm(x_ref, g_ref, n_ref):
    bm = x_ref.shape[0]
    rows = min(NORM_ROWS, bm)
    g = g_ref[...]

    def body(c, carry):
        r0 = pl.multiple_of(c * rows, rows)
        n_ref[pl.ds(r0, rows), :] = _rms_rows(x_ref[pl.ds(r0, rows), :], g).astype(n_ref.dtype)
        return carry

    lax.fori_loop(0, bm // rows, body, 0)


def _dot(a, b):
    return jnp.dot(a, b, preferred_element_type=F32)


def _dot_t(a, b):
    return lax.dot_general(a, b, (((1,), (1,)), ((), ())), preferred_element_type=F32)


def _gelu(x):
    return 0.5 * x * (1.0 + lax.erf(x * (1.0 / math.sqrt(2.0))))


def _norm_mm_kernel(x_ref, g_ref, w_ref, o_ref, n_ref):
    @pl.when(pl.program_id(1) == 0)
    def _():
        _fill_norm(x_ref, g_ref, n_ref)

    o_ref[...] = _dot(n_ref[...], w_ref[...]).astype(o_ref.dtype)


def _layer_spec(block, layer, index_map):
    return pl.BlockSpec((None,) + block, lambda *g: (layer,) + index_map(*g))


def norm_matmul(x, gain, w, layer, *, bm_target=1024, bn_target=1024):
    m, d = x.shape
    n = w.shape[2]
    bm, bn = _pick(m, bm_target), _pick(n, bn_target)
    return pl.pallas_call(
        _norm_mm_kernel,
        grid=(m // bm, n // bn),
        in_specs=[pl.BlockSpec((bm, d), lambda i, j: (i, 0)),
                  pl.BlockSpec((1, d), lambda i, j: (0, 0)),
                  _layer_spec((d, bn), layer, lambda i, j: (0, j))],
        out_specs=pl.BlockSpec((bm, bn), lambda i, j: (i, j)),
        out_shape=jax.ShapeDtypeStruct((m, n), BF16),
        scratch_shapes=[pltpu.VMEM((bm, d), BF16)],
        compiler_params=_params("parallel", "arbitrary"),
        name="norm_matmul",
    )(x, gain.reshape(1, d), w)


def _norm_swiglu_kernel(x_ref, g_ref, wa_ref, wb_ref, o_ref, n_ref):
    @pl.when(pl.program_id(1) == 0)
    def _():
        _fill_norm(x_ref, g_ref, n_ref)

    n = n_ref[...]
    a = _dot(n, wa_ref[...])
    b = _dot(n, wb_ref[...])
    o_ref[...] = (a * jax.nn.sigmoid(a) * b).astype(o_ref.dtype)


def norm_swiglu_in(x, gain, w_in, layer, *, bm_target=1024, bn_target=512):
    m, d = x.shape
    f = w_in.shape[2] // 2
    bm, bn = _pick(m, bm_target), _pick(f, bn_target)
    nb = f // bn
    return pl.pallas_call(
        _norm_swiglu_kernel,
        grid=(m // bm, nb),
        in_specs=[pl.BlockSpec((bm, d), lambda i, j: (i, 0)),
                  pl.BlockSpec((1, d), lambda i, j: (0, 0)),
                  _layer_spec((d, bn), layer, lambda i, j: (0, j)),
                  _layer_spec((d, bn), layer, lambda i, j: (0, j + nb))],
        out_specs=pl.BlockSpec((bm, bn), lambda i, j: (i, j)),
        out_shape=jax.ShapeDtypeStruct((m, f), BF16),
        scratch_shapes=[pltpu.VMEM((bm, d), BF16)],
        compiler_params=_params("parallel", "arbitrary"),
        name="norm_swiglu_in",
    )(x, gain.reshape(1, d), w_in, w_in)


def _mm_res_kernel(x_ref, w_ref, r_ref, o_ref, *, scale):
    o_ref[...] = r_ref[...] + scale * _dot(x_ref[...], w_ref[...])


def matmul_residual(x, w, layer, res, scale, *, bm_target=512, bn_target=1024):
    m, k = x.shape
    n = w.shape[2]
    bm, bn = _pick(m, bm_target), _pick(n, bn_target)
    return pl.pallas_call(
        functools.partial(_mm_res_kernel, scale=scale),
        grid=(n // bn, m // bm),
        in_specs=[pl.BlockSpec((bm, k), lambda j, i: (i, 0)),
                  _layer_spec((k, bn), layer, lambda j, i: (0, j)),
                  pl.BlockSpec((bm, bn), lambda j, i: (i, j))],
        out_specs=pl.BlockSpec((bm, bn), lambda j, i: (i, j)),
        out_shape=jax.ShapeDtypeStruct((m, n), F32),
        compiler_params=_params("parallel", "parallel"),
        name="matmul_residual",
    )(x, w, res)


MIX_FIRST_BLOCK = 3
MIX_N_BLOCKS = 8
(STEP_U, STEP_GV, STEP_CB, STEP_CC, STEP_CX, STEP_Q) = range(6)
(RAW_A, RAW_GV, RAW_CB, RAW_CC) = range(4)


def _mix_in_kernel(x_ref, g_ref, w_ref, sgun_ref, ws_ref, sb_ref, cw_ref,
                   n_ref, qkv_ref, yb_ref, yc_ref,
                   raw_ref, u_ref, gv_ref, carry_ref, *, blocks_per_seq):
    i, j = pl.program_id(0), pl.program_id(1)
    bm, width = u_ref.shape
    n_groups = width // HEAD_DIM
    n_chunks = bm // SGU_CHUNK

    halves = (slice(0, width // 2), slice(width // 2, width))

    def project(cols=slice(None)):
        return _dot(n_ref[...], w_ref[:, cols])

    @pl.when(j == STEP_U)
    def _():
        _fill_norm(x_ref, g_ref, n_ref)

    @pl.when((j == STEP_U) | (j == STEP_CX))
    def _():
        raw_ref[RAW_A] = project()

    @pl.when(j == STEP_U)
    def _():
        u_ref[...] = _gelu(raw_ref[RAW_A])

    @pl.when(j == STEP_GV)
    def _():
        raw_ref[RAW_GV] = project()

    @pl.when(j == STEP_CB)
    def _():
        raw_ref[RAW_CB, :, halves[0]] = project(halves[0])
        gv_ref[...] = _rms_rows(_gelu(raw_ref[RAW_GV]), sgun_ref[...]).astype(gv_ref.dtype)
        raw_ref[RAW_CB, :, halves[1]] = project(halves[1])
        t_idx = lax.broadcasted_iota(jnp.int32, (SGU_CHUNK, SGU_CHUNK), 0)
        s_idx = lax.broadcasted_iota(jnp.int32, (SGU_CHUNK, SGU_CHUNK), 1)
        for g in range(n_groups):
            cols = slice(g * HEAD_DIM, (g + 1) * HEAD_DIM)
            w_tril = jnp.where(s_idx <= t_idx, ws_ref[g], 0.0).astype(BF16)
            xg = jnp.concatenate(
                [gv_ref[c * SGU_CHUNK:(c + 1) * SGU_CHUNK, cols] for c in range(n_chunks)], axis=1)
            mixed = _dot(w_tril, xg) + sb_ref[g]
            for c in range(n_chunks):
                rows = slice(c * SGU_CHUNK, (c + 1) * SGU_CHUNK)
                yb_ref[rows, cols] = (
                    u_ref[rows, cols] * mixed[:, c * HEAD_DIM:(c + 1) * HEAD_DIM]).astype(yb_ref.dtype)

    @pl.when(j == STEP_CC)
    def _():
        raw_ref[RAW_CC] = project()

    @pl.when((j == STEP_CC) & (i % blocks_per_seq == 0))
    def _():
        carry_ref[...] = jnp.zeros_like(carry_ref)

    @pl.when(j >= STEP_Q)
    def _():
        qkv_ref[...] = project().astype(qkv_ref.dtype)

    @pl.when(j == STEP_Q)
    def _():
        z = raw_ref[RAW_CC] * raw_ref[RAW_A]
        prev1 = carry_ref[7:8, :]
        prev2 = carry_ref[6:7, :]
        row = lax.broadcasted_iota(jnp.int32, z.shape, 0)
        z1 = jnp.where(row == 0, prev1, pltpu.roll(z, 1, 0))
        z2 = jnp.where(row == 0, prev2, jnp.where(row == 1, prev1, pltpu.roll(z, 2, 0)))
        conv = cw_ref[0:1, :] * z2 + cw_ref[1:2, :] * z1 + cw_ref[2:3, :] * z
        yc_ref[...] = (raw_ref[RAW_CB] * conv).astype(yc_ref.dtype)
        carry_ref[...] = z[bm - 8:bm, :]


def mixer_in(x, gain, w, layer, sgu_norm, sgu_w_s, sgu_b, conv_w, *, seq, bm_target=512):
    m, d = x.shape
    width = d // 2
    assert w.shape[1] == d and w.shape[2] >= MIX_N_BLOCKS * width
    bm = _pick(seq, bm_target)
    assert bm % SGU_CHUNK == 0 and bm >= 8
    n_groups = width // HEAD_DIM
    const2 = lambda i, j: (0, 0)
    const3 = lambda i, j: (0, 0, 0)
    row_blk = lambda i, j: (i, 0)
    return pl.pallas_call(
        functools.partial(_mix_in_kernel, blocks_per_seq=seq // bm),
        grid=(m // bm, MIX_N_BLOCKS),
        in_specs=[pl.BlockSpec((bm, d), row_blk),
                  pl.BlockSpec((1, d), const2),
                  _layer_spec((d, width), layer, lambda i, j: (0, (j + MIX_FIRST_BLOCK) % MIX_N_BLOCKS)),
                  pl.BlockSpec((1, width), const2),
                  pl.BlockSpec((n_groups, SGU_CHUNK, SGU_CHUNK), const3),
                  pl.BlockSpec((n_groups, SGU_CHUNK, 1), const3),
                  pl.BlockSpec((CONV_TAPS, width), const2)],
        out_specs=[pl.BlockSpec((bm, d), row_blk),
                   pl.BlockSpec((bm, width), lambda i, j: (i, jnp.maximum(j - STEP_Q, 0))),
                   pl.BlockSpec((bm, width), row_blk),
                   pl.BlockSpec((bm, width), row_blk)],
        out_shape=[jax.ShapeDtypeStruct((m, d), BF16),
                   jax.ShapeDtypeStruct((m, 3 * width), BF16),
                   jax.ShapeDtypeStruct((m, width), BF16),
                   jax.ShapeDtypeStruct((m, width), BF16)],
        scratch_shapes=[pltpu.VMEM((4, bm, width), F32),
                        pltpu.VMEM((bm, width), F32),
                        pltpu.VMEM((bm, width), BF16),
                        pltpu.VMEM((8, width), F32)],
        compiler_params=_params("arbitrary", "arbitrary"),
        name="mixer_in",
    )(x, gain.reshape(1, d), w, sgu_norm.reshape(1, width), sgu_w_s,
      sgu_b.reshape(n_groups, SGU_CHUNK, 1), conv_w)


def _bf16_split3(x):
    p1 = x.astype(BF16).astype(F32)
    r1 = x - p1
    p2 = r1.astype(BF16).astype(F32)
    p3 = (r1 - p2).astype(BF16).astype(F32)
    return p1, p2, p3


def _diff_attn_kernel(scal_ref, q_ref, k_ref, v_ref, lam_ref, sub_ref, o_ref, kt_ref, vx_ref,
                      *, n_heads, tq):
    h = pl.program_id(1)
    slope = scal_ref[h]
    lam_init = scal_ref[n_heads]
    half = HEAD_DIM // 2
    seq = q_ref.shape[1]

    pos = lax.broadcasted_iota(jnp.int32, (HEAD_DIM, seq), 1)
    frow = lax.broadcasted_iota(jnp.int32, (HEAD_DIM, seq), 0)
    pos_hi = (pos - pos % HEAD_DIM).astype(F32)
    pos_lo = (pos % HEAD_DIM).astype(F32)
    kt_ref[:HEAD_DIM, :] = k_ref[0].T
    kt_ref[HEAD_DIM:, :] = jnp.where(frow < 3, pos_hi, jnp.where(frow < 6, pos_lo, 0.0)).astype(BF16)
    vx_ref[:, :HEAD_DIM] = v_ref[0]
    vx_ref[:, HEAD_DIM:] = jnp.ones((seq, HEAD_DIM), BF16)

    s1, s2, s3 = _bf16_split3(jnp.full((2 * tq, HEAD_DIM), slope, F32))
    flane = lax.broadcasted_iota(jnp.int32, (2 * tq, HEAD_DIM), 1)
    piece = jnp.where(flane % 3 == 0, s1, jnp.where(flane % 3 == 1, s2, s3))
    q_feat = jnp.where(flane < 6, piece, 0.0).astype(BF16)

    lp = lam_ref[...]
    lam = (jnp.exp(jnp.sum(lp[0:1] * lp[1:2], axis=-1, keepdims=True))
           - jnp.exp(jnp.sum(lp[2:3] * lp[3:4], axis=-1, keepdims=True)) + lam_init)
    sub = sub_ref[...]

    lane = lax.broadcasted_iota(jnp.int32, (tq, HEAD_DIM), 1)
    r_idx = lax.broadcasted_iota(jnp.int32, (2 * tq, tq), 0)
    c_idx = lax.broadcasted_iota(jnp.int32, (2 * tq, tq), 1)
    causal = c_idx <= jnp.where(r_idx >= tq, r_idx - tq, r_idx)

    for i in reversed(range(seq // tq)):
        start, stop = i * tq, (i + 1) * tq
        q = q_ref[0, start:stop, :].astype(F32) * (1.0 / math.sqrt(half))
        qq = jnp.concatenate([jnp.where(lane < half, q, 0.0),
                              jnp.where(lane >= half, q, 0.0)], axis=0).astype(BF16)
        t = _dot(jnp.concatenate([qq, q_feat], axis=1), kt_ref[:, :stop])
        t_diag = jnp.where(causal, t[:, start:], NEG_BIG)
        m = jnp.max(t_diag, axis=-1, keepdims=True)
        if i:
            t_past = t[:, :start]
            m = jnp.maximum(m, jnp.max(t_past, axis=-1, keepdims=True))
            p = jnp.concatenate([jnp.exp(t_past - m), jnp.exp(t_diag - m)], axis=1)
        else:
            p = jnp.exp(t_diag - m)
        a = _dot(p.astype(BF16), vx_ref[:stop, :])
        a = a[:, :HEAD_DIM] / a[:, HEAD_DIM:]
        o = a[:tq] - lam * a[tq:]
        o_ref[0, start:stop, :] = (_rms_rows(o, sub) * (1.0 - lam_init)).astype(o_ref.dtype)


def diff_attention(qkv, lam_params, subln, lam_init, *, batch, seq, tq_target=256):
    width = qkv.shape[1] // 3
    n_heads = width // HEAD_DIM
    tq = _pick(seq, tq_target)
    slopes = 2.0 ** (-8.0 * jnp.arange(1, n_heads + 1, dtype=F32) / n_heads)
    scal = jnp.concatenate([slopes, jnp.full((1,), lam_init, F32)])
    qkv3 = qkv.reshape(batch, seq, 3 * width)
    head_blk = lambda off: pl.BlockSpec((1, seq, HEAD_DIM), lambda b, h: (b, 0, off + h))
    out = pl.pallas_call(
        functools.partial(_diff_attn_kernel, n_heads=n_heads, tq=tq),
        grid=(batch, n_heads),
        in_specs=[pl.BlockSpec(memory_space=pltpu.SMEM),
                  head_blk(0), head_blk(n_heads), head_blk(2 * n_heads),
                  pl.BlockSpec(lam_params.shape, lambda b, h: (0, 0)),
                  pl.BlockSpec((1, HEAD_DIM), lambda b, h: (0, 0))],
        out_specs=head_blk(0),
        out_shape=jax.ShapeDtypeStruct((batch, seq, width), BF16),
        scratch_shapes=[pltpu.VMEM((2 * HEAD_DIM, seq), BF16),
                        pltpu.VMEM((seq, 2 * HEAD_DIM), BF16)],
        compiler_params=_params("parallel", "parallel"),
        name="diff_attention",
    )(scal, qkv3, qkv3, qkv3, lam_params, subln.reshape(1, HEAD_DIM))
    return out.reshape(batch * seq, width)


def _merge_kernel(n_ref, wg0_ref, wg1_ref, wg2_ref, ya_ref, yb_ref, yc_ref,
                  wa_ref, wb_ref, wc_ref, o_ref):
    n = n_ref[...]

    def branch(wg_ref, y_ref, w_ref):
        return jax.nn.sigmoid(_dot(n, wg_ref[...])) * _dot(y_ref[...], w_ref[...])

    merged = branch(wg0_ref, ya_ref, wa_ref) + branch(wg1_ref, yb_ref, wb_ref)
    o_ref[...] = (merged + branch(wg2_ref, yc_ref, wc_ref)).astype(o_ref.dtype)


def gated_merge(n, w_mix_in, layer, y_a, y_b, y_c, w_a, w_b, w_c, *, bm_target=512, bn_target=512):
    m, d = n.shape
    width = y_a.shape[1]
    bm, bn = _pick(m, bm_target), _pick(d, bn_target)
    nb = d // bn
    gate0 = (w_mix_in.shape[2] - N_BRANCHES * d) // bn
    row_blk = lambda i, j: (i, 0)
    col_blk = lambda i, j: (0, j)
    return pl.pallas_call(
        _merge_kernel,
        grid=(m // bm, nb),
        in_specs=[pl.BlockSpec((bm, d), row_blk),
                  _layer_spec((d, bn), layer, lambda i, j: (0, gate0 + j)),
                  _layer_spec((d, bn), layer, lambda i, j: (0, gate0 + nb + j)),
                  _layer_spec((d, bn), layer, lambda i, j: (0, gate0 + 2 * nb + j)),
                  pl.BlockSpec((bm, width), row_blk),
                  pl.BlockSpec((bm, width), row_blk),
                  pl.BlockSpec((bm, width), row_blk),
                  _layer_spec((width, bn), layer, col_blk),
                  _layer_spec((width, bn), layer, col_blk),
                  _layer_spec((width, bn), layer, col_blk)],
        out_specs=pl.BlockSpec((bm, bn), lambda i, j: (i, j)),
        out_shape=jax.ShapeDtypeStruct((m, d), BF16),
        compiler_params=_params("parallel", "arbitrary"),
        name="gated_merge",
    )(n, w_mix_in, w_mix_in, w_mix_in, y_a, y_b, y_c, w_a, w_b, w_c)


def _mem_attn_kernel(x_ref, g_ref, wq_ref, kv_ref, wo_ref, o_ref, n_ref):
    mem_width = wq_ref.shape[1]
    _fill_norm(x_ref, g_ref, n_ref)
    q = _dot(n_ref[...], wq_ref[...]).astype(BF16)
    kv = kv_ref[0]
    scale = 1.0 / math.sqrt(HEAD_DIM)
    heads = []
    for hd in range(mem_width // HEAD_DIM):
        cols = slice(hd * HEAD_DIM, (hd + 1) * HEAD_DIM)
        k_h = kv[:, hd * HEAD_DIM:(hd + 1) * HEAD_DIM]
        v_h = kv[:, mem_width + hd * HEAD_DIM:mem_width + (hd + 1) * HEAD_DIM]
        s = _dot_t(q[:, cols], k_h) * scale
        p = jnp.exp(s - jnp.max(s, axis=-1, keepdims=True))
        denom = jnp.sum(p, axis=-1, keepdims=True)
        heads.append(_dot(p.astype(BF16), v_h) / denom)
    o = jnp.concatenate(heads, axis=1).astype(BF16)
    o_ref[...] = x_ref[...] + _dot(o, wo_ref[...])


def memory_attention(x, gain, w_q, kv, w_o, layer, *, seq, bm_target=512):
    m, d = x.shape
    mem_width = w_q.shape[2]
    bm = _pick(seq, bm_target)
    per_seq = seq // bm
    return pl.pallas_call(
        _mem_attn_kernel,
        grid=(m // bm,),
        in_specs=[pl.BlockSpec((bm, d), lambda i: (i, 0)),
                  pl.BlockSpec((1, d), lambda i: (0, 0)),
                  _layer_spec((d, mem_width), layer, lambda i: (0, 0)),
                  pl.BlockSpec((1,) + kv.shape[1:], lambda i: (i // per_seq, 0, 0)),
                  _layer_spec((mem_width, d), layer, lambda i: (0, 0))],
        out_specs=pl.BlockSpec((bm, d), lambda i: (i, 0)),
        out_shape=jax.ShapeDtypeStruct((m, d), F32),
        scratch_shapes=[pltpu.VMEM((bm, d), BF16)],
        compiler_params=_params("parallel"),
        name="memory_attention",
    )(x, gain.reshape(1, d), w_q, kv, w_o)


def _final_norm_kernel(x_ref, g_ref, o_ref):
    o_ref[...] = _rms_rows(x_ref[...], g_ref[...])


def output_norm(x, gain, *, bm_target=256):
    m, d = x.shape
    bm = _pick(m, bm_target)
    return pl.pallas_call(
        _final_norm_kernel,
        grid=(m // bm,),
        in_specs=[pl.BlockSpec((bm, d), lambda i: (i, 0)),
                  pl.BlockSpec((1, d), lambda i: (0, 0))],
        out_specs=pl.BlockSpec((bm, d), lambda i: (i, 0)),
        out_shape=jax.ShapeDtypeStruct((m, d), F32),
        compiler_params=_params("parallel"),
        name="final_norm",
    )(x, gain.reshape(1, d))


def kernel(x, mem, ffn1_norm, ffn1_w_in, ffn1_w_out, mix_norm, mix_w_in, diff_lambda, diff_subln, diff_w_out, sgu_norm, sgu_w_s, sgu_b, sgu_w_out, conv_w, conv_w_out, mix_w_o, xattn_norm, mem_norm, xattn_w_q, xattn_w_kv, xattn_w_o, ffn2_norm, ffn2_w_in, ffn2_w_out, final_norm):
    batch, seq, d = x.shape
    n_mem = mem.shape[1]
    depth = ffn1_norm.shape[0]
    h = x.reshape(batch * seq, d)
    mem2 = mem.reshape(batch * n_mem, d)

    bf = lambda w: w.astype(BF16)
    ffn1_w_in, ffn1_w_out, ffn2_w_in, ffn2_w_out = map(bf, (ffn1_w_in, ffn1_w_out, ffn2_w_in, ffn2_w_out))
    mix_w_in, diff_w_out, sgu_w_out, conv_w_out, mix_w_o = map(
        bf, (mix_w_in, diff_w_out, sgu_w_out, conv_w_out, mix_w_o))
    xattn_w_q, xattn_w_kv, xattn_w_o = map(bf, (xattn_w_q, xattn_w_kv, xattn_w_o))

    def ffn(h, norm, w_in, w_out, l):
        return matmul_residual(norm_swiglu_in(h, norm, w_in, l), w_out, l, h, 0.5)

    for l in range(depth):
        lam_init = 0.8 - 0.6 * math.exp(-0.3 * l)

        h = ffn(h, ffn1_norm[l], ffn1_w_in, ffn1_w_out, l)

        n_mix, qkv, y_b, y_c = mixer_in(h, mix_norm[l], mix_w_in, l, sgu_norm[l], sgu_w_s[l], sgu_b[l],
                                        conv_w[l], seq=seq)
        y_a = diff_attention(qkv, diff_lambda[l], diff_subln[l], lam_init, batch=batch, seq=seq)
        merged = gated_merge(n_mix, mix_w_in, l, y_a, y_b, y_c, diff_w_out, sgu_w_out, conv_w_out)
        h = matmul_residual(merged, mix_w_o, l, h, 1.0, bm_target=1024)

        kv = norm_matmul(mem2, mem_norm[l], xattn_w_kv, l).reshape(batch, n_mem, -1)
        h = memory_attention(h, xattn_norm[l], xattn_w_q, kv, xattn_w_o, l, seq=seq)

        h = ffn(h, ffn2_norm[l], ffn2_w_in, ffn2_w_out, l)

    return output_norm(h, final_norm).reshape(batch, seq, d)
```

```python
import functools
import math

import jax
import jax.numpy as jnp
from jax import lax
from jax.experimental import pallas as pl
from jax.experimental.pallas import tpu as pltpu

HEAD_DIM = 128
SGU_CHUNK = 128
N_MEM_HEADS = 4
N_BRANCHES = 3
CONV_TAPS = 3
RMS_EPS = 1e-6
NEG_BIG = -1e30
V7X_VMEM_LIMIT_BYTES = 60 * 1024 * 1024
NORM_ROWS = 256

BF16 = jnp.bfloat16
F32 = jnp.float32


def _params(*sem):
    return pltpu.CompilerParams(dimension_semantics=sem, vmem_limit_bytes=V7X_VMEM_LIMIT_BYTES)


def _pick(total, target):
    b = min(total, target)
    while total % b:
        b //= 2
    return b


def _rms_rows(x, g):
    ms = jnp.mean(x * x, axis=-1, keepdims=True)
    return x * lax.rsqrt(ms + RMS_EPS) * g


def _fill_norm(x_ref, g_ref, n_ref):
    bm = x_ref.shape[0]
    rows = min(NORM_ROWS, bm)
    g = g_ref[...]

    def body(c, carry):
        r0 = pl.multiple_of(c * rows, rows)
        n_ref[pl.ds(r0, rows), :] = _rms_rows(x_ref[pl.ds(r0, rows), :], g).astype(n_ref.dtype)
        return carry

    lax.fori_loop(0, bm // rows, body, 0)


def _dot(a, b):
    return jnp.dot(a, b, preferred_element_type=F32)


def _dot_t(a, b):
    return lax.dot_general(a, b, (((1,), (1,)), ((), ())), preferred_element_type=F32)


def _gelu(x):
    return 0.5 * x * (1.0 + lax.erf(x * (1.0 / math.sqrt(2.0))))


def _norm_mm_kernel(x_ref, g_ref, w_ref, o_ref, n_ref):
    @pl.when(pl.program_id(1) == 0)
    def _():
        _fill_norm(x_ref, g_ref, n_ref)

    o_ref[...] = _dot(n_ref[...], w_ref[...]).astype(o_ref.dtype)


def _layer_spec(block, layer, index_map, *, single_buffer=False):
    return pl.BlockSpec((None,) + block, lambda *g: (layer,) + index_map(*g),
                        pipeline_mode=pl.Buffered(1) if single_buffer else None)


def norm_matmul(x, gain, w, layer, *, bm_target=1024, bn_target=1024):
    m, d = x.shape
    n = w.shape[2]
    bm, bn = _pick(m, bm_target), _pick(n, bn_target)
    return pl.pallas_call(
        _norm_mm_kernel,
        grid=(m // bm, n // bn),
        in_specs=[pl.BlockSpec((bm, d), lambda i, j: (i, 0)),
                  pl.BlockSpec((1, d), lambda i, j: (0, 0)),
                  _layer_spec((d, bn), layer, lambda i, j: (0, j))],
        out_specs=pl.BlockSpec((bm, bn), lambda i, j: (i, j)),
        out_shape=jax.ShapeDtypeStruct((m, n), BF16),
        scratch_shapes=[pltpu.VMEM((bm, d), BF16)],
        compiler_params=_params("parallel", "arbitrary"),
        name="norm_matmul",
    )(x, gain.reshape(1, d), w)


def _norm_swiglu_kernel(x_ref, g_ref, wa_ref, wb_ref, o_ref, n_ref):
    @pl.when(pl.program_id(1) == 0)
    def _():
        _fill_norm(x_ref, g_ref, n_ref)

    n = n_ref[...]
    a = _dot(n, wa_ref[...])
    b = _dot(n, wb_ref[...])
    o_ref[...] = (a * jax.nn.sigmoid(a) * b).astype(o_ref.dtype)


def norm_swiglu_in(x, gain, w_in, layer, *, bm_target=1024, bn_target=512):
    m, d = x.shape
    f = w_in.shape[2] // 2
    bm, bn = _pick(m, bm_target), _pick(f, bn_target)
    nb = f // bn
    return pl.pallas_call(
        _norm_swiglu_kernel,
        grid=(m // bm, nb),
        in_specs=[pl.BlockSpec((bm, d), lambda i, j: (i, 0)),
                  pl.BlockSpec((1, d), lambda i, j: (0, 0)),
                  _layer_spec((d, bn), layer, lambda i, j: (0, j)),
                  _layer_spec((d, bn), layer, lambda i, j: (0, j + nb))],
        out_specs=pl.BlockSpec((bm, bn), lambda i, j: (i, j)),
        out_shape=jax.ShapeDtypeStruct((m, f), BF16),
        scratch_shapes=[pltpu.VMEM((bm, d), BF16)],
        compiler_params=_params("parallel", "arbitrary"),
        name="norm_swiglu_in",
    )(x, gain.reshape(1, d), w_in, w_in)


def _swiglu_kernel(n_ref, wa_ref, wb_ref, o_ref):
    n = n_ref[...]
    a = _dot(n, wa_ref[...])
    b = _dot(n, wb_ref[...])
    o_ref[...] = (a * jax.nn.sigmoid(a) * b).astype(o_ref.dtype)


def swiglu_in(n, w_in, layer, *, bm_target=2048, bn_target=512):
    m, d = n.shape
    f = w_in.shape[2] // 2
    bm, bn = _pick(m, bm_target), _pick(f, bn_target)
    nb = f // bn
    return pl.pallas_call(
        _swiglu_kernel,
        grid=(m // bm, nb),
        in_specs=[pl.BlockSpec((bm, d), lambda i, j: (i, 0)),
                  _layer_spec((d, bn), layer, lambda i, j: (0, j)),
                  _layer_spec((d, bn), layer, lambda i, j: (0, j + nb))],
        out_specs=pl.BlockSpec((bm, bn), lambda i, j: (i, j)),
        out_shape=jax.ShapeDtypeStruct((m, f), BF16),
        compiler_params=_params("parallel", "parallel"),
        name="swiglu_in",
    )(n, w_in, w_in)


def _mm_res_kernel(x_ref, w_ref, r_ref, o_ref, *, scale):
    o_ref[...] = r_ref[...] + scale * _dot(x_ref[...], w_ref[...])


def matmul_residual(x, w, layer, res, scale, *, bm_target=1024, bn_target=1024):
    m, k = x.shape
    n = w.shape[2]
    bm, bn = _pick(m, bm_target), _pick(n, bn_target)
    return pl.pallas_call(
        functools.partial(_mm_res_kernel, scale=scale),
        grid=(n // bn, m // bm),
        in_specs=[pl.BlockSpec((bm, k), lambda j, i: (i, 0)),
                  _layer_spec((k, bn), layer, lambda j, i: (0, j), single_buffer=True),
                  pl.BlockSpec((bm, bn), lambda j, i: (i, j))],
        out_specs=pl.BlockSpec((bm, bn), lambda j, i: (i, j)),
        out_shape=jax.ShapeDtypeStruct((m, n), F32),
        compiler_params=_params("parallel", "parallel"),
        name="matmul_residual",
    )(x, w, res)


MIX_PAIRS = 4
MIX_FIRST_PAIR = 1
(STEP_VU, STEP_GB, STEP_CX, STEP_QK) = range(MIX_PAIRS)


def _mix_in_kernel(x_ref, g_ref, w_ref, sgun_ref, ws_ref, sb_ref, cw_ref,
                   n_ref, qk_ref, v_ref, yb_ref, yc_ref,
                   u_ref, gv_ref, cb_ref, z_ref, carry_ref, *, blocks_per_seq):
    i, j = pl.program_id(0), pl.program_id(1)
    bm, width = u_ref.shape
    n_groups = width // HEAD_DIM
    n_chunks = bm // SGU_CHUNK
    first, second = slice(0, width), slice(width, 2 * width)

    def project(cols=slice(None)):
        return _dot(n_ref[...], w_ref[:, cols])

    @pl.when(j == STEP_VU)
    def _():
        _fill_norm(x_ref, g_ref, n_ref)

    @pl.when(j == STEP_VU)
    def _():
        v_ref[...] = project(first).astype(v_ref.dtype)
        u_ref[...] = _gelu(project(second))

    @pl.when(j == STEP_GB)
    def _():
        gv_ref[...] = _rms_rows(_gelu(project(first)), sgun_ref[...]).astype(gv_ref.dtype)
        cb_ref[...] = project(second)

    @pl.when(j == STEP_CX)
    def _():
        t_idx = lax.broadcasted_iota(jnp.int32, (SGU_CHUNK, SGU_CHUNK), 0)
        s_idx = lax.broadcasted_iota(jnp.int32, (SGU_CHUNK, SGU_CHUNK), 1)

        def sgu_groups(groups):
            for g in groups:
                cols = slice(g * HEAD_DIM, (g + 1) * HEAD_DIM)
                w_tril = jnp.where(s_idx <= t_idx, ws_ref[g], 0.0).astype(BF16)
                xg = jnp.concatenate(
                    [gv_ref[c * SGU_CHUNK:(c + 1) * SGU_CHUNK, cols] for c in range(n_chunks)], axis=1)
                mixed = _dot(w_tril, xg) + sb_ref[g]
                for c in range(n_chunks):
                    rows = slice(c * SGU_CHUNK, (c + 1) * SGU_CHUNK)
                    yb_ref[rows, cols] = (
                        u_ref[rows, cols] * mixed[:, c * HEAD_DIM:(c + 1) * HEAD_DIM]).astype(yb_ref.dtype)

        z_ref[...] = project(first)
        sgu_groups(range(n_groups // 2))
        z_ref[...] = z_ref[...] * project(second)
        sgu_groups(range(n_groups // 2, n_groups))

    @pl.when((j == STEP_CX) & (i % blocks_per_seq == 0))
    def _():
        carry_ref[...] = jnp.zeros_like(carry_ref)

    @pl.when(j == STEP_QK)
    def _():
        qk_ref[...] = project().astype(qk_ref.dtype)

    @pl.when(j > STEP_CX)
    def _():
        z = z_ref[...]
        prev1 = carry_ref[7:8, :]
        prev2 = carry_ref[6:7, :]
        row = lax.broadcasted_iota(jnp.int32, z.shape, 0)
        z1 = jnp.where(row == 0, prev1, pltpu.roll(z, 1, 0))
        z2 = jnp.where(row == 0, prev2, jnp.where(row == 1, prev1, pltpu.roll(z, 2, 0)))
        conv = cw_ref[0:1, :] * z2 + cw_ref[1:2, :] * z1 + cw_ref[2:3, :] * z
        yc_ref[...] = (cb_ref[...] * conv).astype(yc_ref.dtype)
        carry_ref[...] = z[bm - 8:bm, :]


def mixer_in(x, gain, w, layer, sgu_norm, sgu_w_s, sgu_b, conv_w, *, seq, bm_target=512):
    m, d = x.shape
    width = d // 2
    assert w.shape[1] == d and w.shape[2] >= 2 * MIX_PAIRS * width
    bm = _pick(seq, bm_target)
    assert bm % SGU_CHUNK == 0 and bm >= 8
    n_groups = width // HEAD_DIM
    const2 = lambda i, j: (0, 0)
    const3 = lambda i, j: (0, 0, 0)
    row_blk = lambda i, j: (i, 0)
    return pl.pallas_call(
        functools.partial(_mix_in_kernel, blocks_per_seq=seq // bm),
        grid=(m // bm, MIX_PAIRS),
        in_specs=[pl.BlockSpec((bm, d), row_blk),
                  pl.BlockSpec((1, d), const2),
                  _layer_spec((d, 2 * width), layer, lambda i, j: (0, (j + MIX_FIRST_PAIR) % MIX_PAIRS)),
                  pl.BlockSpec((1, width), const2),
                  pl.BlockSpec((n_groups, SGU_CHUNK, SGU_CHUNK), const3),
                  pl.BlockSpec((n_groups, SGU_CHUNK, 1), const3),
                  pl.BlockSpec((CONV_TAPS, width), const2)],
        out_specs=[pl.BlockSpec((bm, d), row_blk),
                   pl.BlockSpec((bm, 2 * width), row_blk),
                   pl.BlockSpec((bm, width), row_blk),
                   pl.BlockSpec((bm, width), row_blk),
                   pl.BlockSpec((bm, width), row_blk)],
        out_shape=[jax.ShapeDtypeStruct((m, d), BF16),
                   jax.ShapeDtypeStruct((m, 2 * width), BF16),
                   jax.ShapeDtypeStruct((m, width), BF16),
                   jax.ShapeDtypeStruct((m, width), BF16),
                   jax.ShapeDtypeStruct((m, width), BF16)],
        scratch_shapes=[pltpu.VMEM((bm, width), F32),
                        pltpu.VMEM((bm, width), BF16),
                        pltpu.VMEM((bm, width), F32),
                        pltpu.VMEM((bm, width), F32),
                        pltpu.VMEM((8, width), F32)],
        compiler_params=_params("arbitrary", "arbitrary"),
        name="mixer_in",
    )(x, gain.reshape(1, d), w, sgu_norm.reshape(1, width), sgu_w_s,
      sgu_b.reshape(n_groups, SGU_CHUNK, 1), conv_w)


def _bf16_split3(x):
    p1 = x.astype(BF16).astype(F32)
    r1 = x - p1
    p2 = r1.astype(BF16).astype(F32)
    p3 = (r1 - p2).astype(BF16).astype(F32)
    return p1, p2, p3


def _diff_attn_kernel(scal_ref, q_ref, k_ref, v_ref, lam_ref, sub_ref, o_ref, kt_ref, vx_ref,
                      *, n_heads, tq):
    heads_here = kt_ref.shape[0]
    lam_init = scal_ref[n_heads]
    half = HEAD_DIM // 2
    seq = q_ref.shape[1]

    pos = lax.broadcasted_iota(jnp.int32, (HEAD_DIM, seq), 1)
    frow = lax.broadcasted_iota(jnp.int32, (HEAD_DIM, seq), 0)
    pos_hi = (pos - pos % HEAD_DIM).astype(F32)
    pos_lo = (pos % HEAD_DIM).astype(F32)
    pos_rows = jnp.where(frow < 3, pos_hi, jnp.where(frow < 6, pos_lo, 0.0)).astype(BF16)
    for hh in range(heads_here):
        cols = slice(hh * HEAD_DIM, (hh + 1) * HEAD_DIM)
        kt_ref[hh, :HEAD_DIM, :] = k_ref[0, :, cols].T
        kt_ref[hh, HEAD_DIM:, :] = pos_rows
        vx_ref[hh, :, :HEAD_DIM] = v_ref[0, :, cols]
        vx_ref[hh, :, HEAD_DIM:] = jnp.ones((seq, HEAD_DIM), BF16)

    lp = lam_ref[...]
    lam = (jnp.exp(jnp.sum(lp[0:1] * lp[1:2], axis=-1, keepdims=True))
           - jnp.exp(jnp.sum(lp[2:3] * lp[3:4], axis=-1, keepdims=True)) + lam_init)
    sub = sub_ref[...]

    lane = lax.broadcasted_iota(jnp.int32, (tq, HEAD_DIM), 1)
    flane = lax.broadcasted_iota(jnp.int32, (2 * tq, HEAD_DIM), 1)
    r_idx = lax.broadcasted_iota(jnp.int32, (2 * tq, tq), 0)
    c_idx = lax.broadcasted_iota(jnp.int32, (2 * tq, tq), 1)
    causal = c_idx <= jnp.where(r_idx >= tq, r_idx - tq, r_idx)

    for hh in range(heads_here):
        cols = slice(hh * HEAD_DIM, (hh + 1) * HEAD_DIM)
        slope = scal_ref[pl.program_id(1) * heads_here + hh]
        s1, s2, s3 = _bf16_split3(jnp.full((2 * tq, HEAD_DIM), slope, F32))
        piece = jnp.where(flane % 3 == 0, s1, jnp.where(flane % 3 == 1, s2, s3))
        q_feat = jnp.where(flane < 6, piece, 0.0).astype(BF16)

        for i in reversed(range(seq // tq)):
            start, stop = i * tq, (i + 1) * tq
            q = q_ref[0, start:stop, cols].astype(F32) * (1.0 / math.sqrt(half))
            qq = jnp.concatenate([jnp.where(lane < half, q, 0.0),
                                  jnp.where(lane >= half, q, 0.0)], axis=0).astype(BF16)
            t = _dot(jnp.concatenate([qq, q_feat], axis=1), kt_ref[hh, :, :stop])
            t_diag = jnp.where(causal, t[:, start:], NEG_BIG)
            m = jnp.max(t_diag, axis=-1, keepdims=True)
            if i:
                t_past = t[:, :start]
                m = jnp.maximum(m, jnp.max(t_past, axis=-1, keepdims=True))
                p = jnp.concatenate([jnp.exp(t_past - m), jnp.exp(t_diag - m)], axis=1)
            else:
                p = jnp.exp(t_diag - m)
            a = _dot(p.astype(BF16), vx_ref[hh, :stop, :])
            a = a[:, :HEAD_DIM] / a[:, HEAD_DIM:]
            o = a[:tq] - lam * a[tq:]
            o_ref[0, start:stop, cols] = (_rms_rows(o, sub) * (1.0 - lam_init)).astype(o_ref.dtype)


def diff_attention(qk, v, lam_params, subln, lam_init, *, batch, seq, tq_target=256):
    width = v.shape[1]
    n_heads = width // HEAD_DIM
    tq = _pick(seq, tq_target)
    slopes = 2.0 ** (-8.0 * jnp.arange(1, n_heads + 1, dtype=F32) / n_heads)
    scal = jnp.concatenate([slopes, jnp.full((1,), lam_init, F32)])
    qk3 = qk.reshape(batch, seq, 2 * width)
    v3 = v.reshape(batch, seq, width)
    hps = 2 if n_heads % 2 == 0 else 1
    groups = n_heads // hps
    head_blk = lambda off: pl.BlockSpec((1, seq, hps * HEAD_DIM), lambda b, h: (b, 0, off + h))
    out = pl.pallas_call(
        functools.partial(_diff_attn_kernel, n_heads=n_heads, tq=tq),
        grid=(batch, groups),
        in_specs=[pl.BlockSpec(memory_space=pltpu.SMEM),
                  head_blk(0), head_blk(groups), head_blk(0),
                  pl.BlockSpec(lam_params.shape, lambda b, h: (0, 0)),
                  pl.BlockSpec((1, HEAD_DIM), lambda b, h: (0, 0))],
        out_specs=head_blk(0),
        out_shape=jax.ShapeDtypeStruct((batch, seq, width), BF16),
        scratch_shapes=[pltpu.VMEM((hps, 2 * HEAD_DIM, seq), BF16),
                        pltpu.VMEM((hps, seq, 2 * HEAD_DIM), BF16)],
        compiler_params=_params("parallel", "parallel"),
        name="diff_attention",
    )(scal, qk3, qk3, v3, lam_params, subln.reshape(1, HEAD_DIM))
    return out.reshape(batch * seq, width)


def _merge_kernel(n_ref, wg0_ref, wg1_ref, wg2_ref, ya_ref, yb_ref, yc_ref,
                  wa_ref, wb_ref, wc_ref, o_ref):
    n = n_ref[...]

    def branch(wg_ref, y_ref, w_ref):
        return jax.nn.sigmoid(_dot(n, wg_ref[...])) * _dot(y_ref[...], w_ref[...])

    merged = branch(wg0_ref, ya_ref, wa_ref) + branch(wg1_ref, yb_ref, wb_ref)
    o_ref[...] = (merged + branch(wg2_ref, yc_ref, wc_ref)).astype(o_ref.dtype)


def gated_merge(n, w_mix_in, layer, y_a, y_b, y_c, w_a, w_b, w_c, *, bm_target=1024, bn_target=512):
    m, d = n.shape
    width = y_a.shape[1]
    bm, bn = _pick(m, bm_target), _pick(d, bn_target)
    nb = d // bn
    gate0 = (w_mix_in.shape[2] - N_BRANCHES * d) // bn
    row_blk = lambda i, j: (i, 0)
    col_blk = lambda i, j: (0, j)
    return pl.pallas_call(
        _merge_kernel,
        grid=(m // bm, nb),
        in_specs=[pl.BlockSpec((bm, d), row_blk),
                  _layer_spec((d, bn), layer, lambda i, j: (0, gate0 + j)),
                  _layer_spec((d, bn), layer, lambda i, j: (0, gate0 + nb + j)),
                  _layer_spec((d, bn), layer, lambda i, j: (0, gate0 + 2 * nb + j)),
                  pl.BlockSpec((bm, width), row_blk),
                  pl.BlockSpec((bm, width), row_blk),
                  pl.BlockSpec((bm, width), row_blk),
                  _layer_spec((width, bn), layer, col_blk),
                  _layer_spec((width, bn), layer, col_blk),
                  _layer_spec((width, bn), layer, col_blk)],
        out_specs=pl.BlockSpec((bm, bn), lambda i, j: (i, j)),
        out_shape=jax.ShapeDtypeStruct((m, d), BF16),
        compiler_params=_params("parallel", "arbitrary"),
        name="gated_merge",
    )(n, w_mix_in, w_mix_in, w_mix_in, y_a, y_b, y_c, w_a, w_b, w_c)


def _mem_attn_kernel(x_ref, g_ref, wq_ref, kv_ref, wo_ref, g_next_ref, o_ref, n_next_ref, n_ref):
    mem_width = wq_ref.shape[1]
    _fill_norm(x_ref, g_ref, n_ref)
    q = _dot(n_ref[...], wq_ref[...]).astype(BF16)
    kv = kv_ref[0]
    scale = 1.0 / math.sqrt(HEAD_DIM)
    heads = []
    for hd in range(mem_width // HEAD_DIM):
        cols = slice(hd * HEAD_DIM, (hd + 1) * HEAD_DIM)
        k_h = kv[:, hd * HEAD_DIM:(hd + 1) * HEAD_DIM]
        v_h = kv[:, mem_width + hd * HEAD_DIM:mem_width + (hd + 1) * HEAD_DIM]
        s = _dot_t(q[:, cols], k_h) * scale
        p = jnp.exp(s - jnp.max(s, axis=-1, keepdims=True))
        denom = jnp.sum(p, axis=-1, keepdims=True)
        heads.append(_dot(p.astype(BF16), v_h) / denom)
    o = jnp.concatenate(heads, axis=1).astype(BF16)
    o_ref[...] = x_ref[...] + _dot(o, wo_ref[...])
    _fill_norm(o_ref, g_next_ref, n_next_ref)


def memory_attention(x, gain, w_q, kv, w_o, layer, gain_next, *, seq, bm_target=512):
    m, d = x.shape
    mem_width = w_q.shape[2]
    bm = _pick(seq, bm_target)
    per_seq = seq // bm
    return pl.pallas_call(
        _mem_attn_kernel,
        grid=(m // bm,),
        in_specs=[pl.BlockSpec((bm, d), lambda i: (i, 0)),
                  pl.BlockSpec((1, d), lambda i: (0, 0)),
                  _layer_spec((d, mem_width), layer, lambda i: (0, 0)),
                  pl.BlockSpec((1,) + kv.shape[1:], lambda i: (i // per_seq, 0, 0)),
                  _layer_spec((mem_width, d), layer, lambda i: (0, 0)),
                  pl.BlockSpec((1, d), lambda i: (0, 0))],
        out_specs=[pl.BlockSpec((bm, d), lambda i: (i, 0)),
                   pl.BlockSpec((bm, d), lambda i: (i, 0))],
        out_shape=[jax.ShapeDtypeStruct((m, d), F32),
                   jax.ShapeDtypeStruct((m, d), BF16)],
        scratch_shapes=[pltpu.VMEM((bm, d), BF16)],
        compiler_params=_params("parallel"),
        name="memory_attention",
    )(x, gain.reshape(1, d), w_q, kv, w_o, gain_next.reshape(1, d))


def _final_norm_kernel(x_ref, g_ref, o_ref):
    o_ref[...] = _rms_rows(x_ref[...], g_ref[...])


def output_norm(x, gain, *, bm_target=256):
    m, d = x.shape
    bm = _pick(m, bm_target)
    return pl.pallas_call(
        _final_norm_kernel,
        grid=(m // bm,),
        in_specs=[pl.BlockSpec((bm, d), lambda i: (i, 0)),
                  pl.BlockSpec((1, d), lambda i: (0, 0))],
        out_specs=pl.BlockSpec((bm, d), lambda i: (i, 0)),
        out_shape=jax.ShapeDtypeStruct((m, d), F32),
        compiler_params=_params("parallel"),
        name="final_norm",
    )(x, gain.reshape(1, d))


def kernel(x, mem, ffn1_norm, ffn1_w_in, ffn1_w_out, mix_norm, mix_w_in, diff_lambda, diff_subln, diff_w_out, sgu_norm, sgu_w_s, sgu_b, sgu_w_out, conv_w, conv_w_out, mix_w_o, xattn_norm, mem_norm, xattn_w_q, xattn_w_kv, xattn_w_o, ffn2_norm, ffn2_w_in, ffn2_w_out, final_norm):
    batch, seq, d = x.shape
    n_mem = mem.shape[1]
    depth = ffn1_norm.shape[0]
    h = x.reshape(batch * seq, d)
    mem2 = mem.reshape(batch * n_mem, d)

    bf = lambda w: w.astype(BF16)
    ffn1_w_in, ffn1_w_out, ffn2_w_in, ffn2_w_out = map(bf, (ffn1_w_in, ffn1_w_out, ffn2_w_in, ffn2_w_out))
    mix_w_in, diff_w_out, sgu_w_out, conv_w_out, mix_w_o = map(
        bf, (mix_w_in, diff_w_out, sgu_w_out, conv_w_out, mix_w_o))
    xattn_w_q, xattn_w_kv, xattn_w_o = map(bf, (xattn_w_q, xattn_w_kv, xattn_w_o))

    for l in range(depth):
        lam_init = 0.8 - 0.6 * math.exp(-0.3 * l)

        h = matmul_residual(norm_swiglu_in(h, ffn1_norm[l], ffn1_w_in, l), ffn1_w_out, l, h, 0.5)

        n_mix, qk, v, y_b, y_c = mixer_in(h, mix_norm[l], mix_w_in, l, sgu_norm[l], sgu_w_s[l], sgu_b[l],
                                          conv_w[l], seq=seq)
        y_a = diff_attention(qk, v, diff_lambda[l], diff_subln[l], lam_init, batch=batch, seq=seq)
        merged = gated_merge(n_mix, mix_w_in, l, y_a, y_b, y_c, diff_w_out, sgu_w_out, conv_w_out)
        h = matmul_residual(merged, mix_w_o, l, h, 1.0)

        kv = norm_matmul(mem2, mem_norm[l], xattn_w_kv, l).reshape(batch, n_mem, -1)
        h, n_ffn2 = memory_attention(h, xattn_norm[l], xattn_w_q, kv, xattn_w_o, l, ffn2_norm[l], seq=seq)

        h = matmul_residual(swiglu_in(n_ffn2, ffn2_w_in, l), ffn2_w_out, l, h, 0.5)

    return output_norm(h, final_norm).reshape(batch, seq, d)
```

```python
import functools
import math

import jax
import jax.numpy as jnp
from jax import lax
from jax.experimental import pallas as pl
from jax.experimental.pallas import tpu as pltpu

HEAD_DIM = 128
SGU_CHUNK = 128
N_MEM_HEADS = 4
N_BRANCHES = 3
CONV_TAPS = 3
RMS_EPS = 1e-6
NEG_BIG = -1e30
V7X_VMEM_LIMIT_BYTES = 60 * 1024 * 1024
NORM_ROWS = 256

BF16 = jnp.bfloat16
F32 = jnp.float32


def _params(*sem):
    return pltpu.CompilerParams(dimension_semantics=sem, vmem_limit_bytes=V7X_VMEM_LIMIT_BYTES)


def _pick(total, target):
    b = min(total, target)
    while total % b:
        b //= 2
    return b


def _rms_rows(x, g):
    ms = jnp.mean(x * x, axis=-1, keepdims=True)
    return x * lax.rsqrt(ms + RMS_EPS) * g


def _fill_norm(x_ref, g_ref, n_ref):
    bm = x_ref.shape[0]
    rows = min(NORM_ROWS, bm)
    g = g_ref[...]

    def body(c, carry):
        r0 = pl.multiple_of(c * rows, rows)
        n_ref[pl.ds(r0, rows), :] = _rms_rows(x_ref[pl.ds(r0, rows), :], g).astype(n_ref.dtype)
        return carry

    lax.fori_loop(0, bm // rows, body, 0)


def _dot(a, b):
    return jnp.dot(a, b, preferred_element_type=F32)


def _dot_t(a, b):
    return lax.dot_general(a, b, (((1,), (1,)), ((), ())), preferred_element_type=F32)


def _gelu(x):
    return 0.5 * x * (1.0 + lax.erf(x * (1.0 / math.sqrt(2.0))))


def _norm_mm_kernel(x_ref, g_ref, w_ref, o_ref, n_ref):
    @pl.when(pl.program_id(1) == 0)
    def _():
        _fill_norm(x_ref, g_ref, n_ref)

    o_ref[...] = _dot(n_ref[...], w_ref[...]).astype(o_ref.dtype)


def _layer_spec(block, layer, index_map, *, single_buffer=False):
    return pl.BlockSpec((None,) + block, lambda *g: (layer,) + index_map(*g),
                        pipeline_mode=pl.Buffered(1) if single_buffer else None)


def norm_matmul(x, gain, w, layer, *, bm_target=1024, bn_target=1024):
    m, d = x.shape
    n = w.shape[2]
    bm, bn = _pick(m, bm_target), _pick(n, bn_target)
    return pl.pallas_call(
        _norm_mm_kernel,
        grid=(m // bm, n // bn),
        in_specs=[pl.BlockSpec((bm, d), lambda i, j: (i, 0)),
                  pl.BlockSpec((1, d), lambda i, j: (0, 0)),
                  _layer_spec((d, bn), layer, lambda i, j: (0, j))],
        out_specs=pl.BlockSpec((bm, bn), lambda i, j: (i, j)),
        out_shape=jax.ShapeDtypeStruct((m, n), BF16),
        scratch_shapes=[pltpu.VMEM((bm, d), BF16)],
        compiler_params=_params("parallel", "arbitrary"),
        name="norm_matmul",
    )(x, gain.reshape(1, d), w)


def _norm_swiglu_kernel(x_ref, g_ref, wa_ref, wb_ref, o_ref, n_ref):
    @pl.when(pl.program_id(1) == 0)
    def _():
        _fill_norm(x_ref, g_ref, n_ref)

    n = n_ref[...]
    a = _dot(n, wa_ref[...])
    b = _dot(n, wb_ref[...])
    o_ref[...] = (a * jax.nn.sigmoid(a) * b).astype(o_ref.dtype)


def norm_swiglu_in(x, gain, w_in, layer, *, bm_target=1024, bn_target=512):
    m, d = x.shape
    f = w_in.shape[2] // 2
    bm, bn = _pick(m, bm_target), _pick(f, bn_target)
    nb = f // bn
    return pl.pallas_call(
        _norm_swiglu_kernel,
        grid=(m // bm, nb),
        in_specs=[pl.BlockSpec((bm, d), lambda i, j: (i, 0)),
                  pl.BlockSpec((1, d), lambda i, j: (0, 0)),
                  _layer_spec((d, bn), layer, lambda i, j: (0, j)),
                  _layer_spec((d, bn), layer, lambda i, j: (0, j + nb))],
        out_specs=pl.BlockSpec((bm, bn), lambda i, j: (i, j)),
        out_shape=jax.ShapeDtypeStruct((m, f), BF16),
        scratch_shapes=[pltpu.VMEM((bm, d), BF16)],
        compiler_params=_params("parallel", "arbitrary"),
        name="norm_swiglu_in",
    )(x, gain.reshape(1, d), w_in, w_in)


def _swiglu_kernel(n_ref, wa_ref, wb_ref, o_ref):
    n = n_ref[...]
    a = _dot(n, wa_ref[...])
    b = _dot(n, wb_ref[...])
    o_ref[...] = (a * jax.nn.sigmoid(a) * b).astype(o_ref.dtype)


def swiglu_in(n, w_in, layer, *, bm_target=1024, bn_target=512):
    m, d = n.shape
    f = w_in.shape[2] // 2
    bm, bn = _pick(m, bm_target), _pick(f, bn_target)
    nb = f // bn
    return pl.pallas_call(
        _swiglu_kernel,
        grid=(m // bm, nb),
        in_specs=[pl.BlockSpec((bm, d), lambda i, j: (i, 0)),
                  _layer_spec((d, bn), layer, lambda i, j: (0, j)),
                  _layer_spec((d, bn), layer, lambda i, j: (0, j + nb))],
        out_specs=pl.BlockSpec((bm, bn), lambda i, j: (i, j)),
        out_shape=jax.ShapeDtypeStruct((m, f), BF16),
        compiler_params=_params("parallel", "parallel"),
        name="swiglu_in",
    )(n, w_in, w_in)


def _mm_res_norm_kernel(x_ref, w_ref, r_ref, g_ref, *out_refs, scale, final):
    h = r_ref[...] + scale * _dot(x_ref[...], w_ref[...])
    normed = _rms_rows(h, g_ref[...])
    if final:
        out_refs[0][...] = normed
    else:
        out_refs[0][...] = h
        out_refs[1][...] = normed.astype(out_refs[1].dtype)


def matmul_residual_norm(x, w, layer, res, scale, gain, *, final=False, bm_target=256):
    m, k = x.shape
    n = w.shape[2]
    bm = _pick(m, bm_target)
    row_blk = lambda i: (i, 0)
    out_specs = [pl.BlockSpec((bm, n), row_blk)]
    out_shape = [jax.ShapeDtypeStruct((m, n), F32)]
    if not final:
        out_specs.append(pl.BlockSpec((bm, n), row_blk))
        out_shape.append(jax.ShapeDtypeStruct((m, n), BF16))
    out = pl.pallas_call(
        functools.partial(_mm_res_norm_kernel, scale=scale, final=final),
        grid=(m // bm,),
        in_specs=[pl.BlockSpec((bm, k), row_blk),
                  _layer_spec((k, n), layer, lambda i: (0, 0), single_buffer=True),
                  pl.BlockSpec((bm, n), row_blk),
                  pl.BlockSpec((1, n), lambda i: (0, 0))],
        out_specs=out_specs,
        out_shape=out_shape,
        compiler_params=_params("parallel"),
        name="matmul_residual_norm",
    )(x, w, res, gain.reshape(1, n))
    return out[0] if final else out


MIX_PAIRS = 4
MIX_FIRST_PAIR = 1
(STEP_VU, STEP_GB, STEP_CX, STEP_QK) = range(MIX_PAIRS)


def _mix_in_kernel(n_ref, w_ref, sgun_ref, ws_ref, sb_ref, cw_ref,
                   qk_ref, v_ref, yb_ref, yc_ref,
                   u_ref, gv_ref, cb_ref, z_ref, carry_ref, *, blocks_per_seq):
    i, j = pl.program_id(0), pl.program_id(1)
    bm, width = u_ref.shape
    n_groups = width // HEAD_DIM
    n_chunks = bm // SGU_CHUNK
    first, second = slice(0, width), slice(width, 2 * width)

    def project(cols=slice(None)):
        return _dot(n_ref[...], w_ref[:, cols])

    @pl.when(j == STEP_VU)
    def _():
        v_ref[...] = project(first).astype(v_ref.dtype)
        u_ref[...] = _gelu(project(second))

    @pl.when(j == STEP_GB)
    def _():
        gv_ref[...] = _rms_rows(_gelu(project(first)), sgun_ref[...]).astype(gv_ref.dtype)
        cb_ref[...] = project(second)

    @pl.when(j == STEP_CX)
    def _():
        t_idx = lax.broadcasted_iota(jnp.int32, (SGU_CHUNK, SGU_CHUNK), 0)
        s_idx = lax.broadcasted_iota(jnp.int32, (SGU_CHUNK, SGU_CHUNK), 1)

        def sgu_groups(groups):
            for g in groups:
                cols = slice(g * HEAD_DIM, (g + 1) * HEAD_DIM)
                w_tril = jnp.where(s_idx <= t_idx, ws_ref[g], 0.0).astype(BF16)
                xg = jnp.concatenate(
                    [gv_ref[c * SGU_CHUNK:(c + 1) * SGU_CHUNK, cols] for c in range(n_chunks)], axis=1)
                mixed = _dot(w_tril, xg) + sb_ref[g]
                for c in range(n_chunks):
                    rows = slice(c * SGU_CHUNK, (c + 1) * SGU_CHUNK)
                    yb_ref[rows, cols] = (
                        u_ref[rows, cols] * mixed[:, c * HEAD_DIM:(c + 1) * HEAD_DIM]).astype(yb_ref.dtype)

        z_ref[...] = project(first)
        sgu_groups(range(n_groups // 2))
        z_ref[...] = z_ref[...] * project(second)
        sgu_groups(range(n_groups // 2, n_groups))

    @pl.when((j == STEP_CX) & (i % blocks_per_seq == 0))
    def _():
        carry_ref[...] = jnp.zeros_like(carry_ref)

    @pl.when(j == STEP_QK)
    def _():
        qk_ref[...] = project().astype(qk_ref.dtype)

    @pl.when(j > STEP_CX)
    def _():
        z = z_ref[...]
        prev1 = carry_ref[7:8, :]
        prev2 = carry_ref[6:7, :]
        row = lax.broadcasted_iota(jnp.int32, z.shape, 0)
        z1 = jnp.where(row == 0, prev1, pltpu.roll(z, 1, 0))
        z2 = jnp.where(row == 0, prev2, jnp.where(row == 1, prev1, pltpu.roll(z, 2, 0)))
        conv = cw_ref[0:1, :] * z2 + cw_ref[1:2, :] * z1 + cw_ref[2:3, :] * z
        yc_ref[...] = (cb_ref[...] * conv).astype(yc_ref.dtype)
        carry_ref[...] = z[bm - 8:bm, :]


def mixer_in(n, w, layer, sgu_norm, sgu_w_s, sgu_b, conv_w, *, seq, bm_target=512):
    m, d = n.shape
    width = d // 2
    assert w.shape[1] == d and w.shape[2] >= 2 * MIX_PAIRS * width
    bm = _pick(seq, bm_target)
    assert bm % SGU_CHUNK == 0 and bm >= 8
    n_groups = width // HEAD_DIM
    const2 = lambda i, j: (0, 0)
    const3 = lambda i, j: (0, 0, 0)
    row_blk = lambda i, j: (i, 0)
    return pl.pallas_call(
        functools.partial(_mix_in_kernel, blocks_per_seq=seq // bm),
        grid=(m // bm, MIX_PAIRS),
        in_specs=[pl.BlockSpec((bm, d), row_blk),
                  _layer_spec((d, 2 * width), layer, lambda i, j: (0, (j + MIX_FIRST_PAIR) % MIX_PAIRS)),
                  pl.BlockSpec((1, width), const2),
                  pl.BlockSpec((n_groups, SGU_CHUNK, SGU_CHUNK), const3),
                  pl.BlockSpec((n_groups, SGU_CHUNK, 1), const3),
                  pl.BlockSpec((CONV_TAPS, width), const2)],
        out_specs=[pl.BlockSpec((bm, 2 * width), row_blk),
                   pl.BlockSpec((bm, width), row_blk),
                   pl.BlockSpec((bm, width), row_blk),
                   pl.BlockSpec((bm, width), row_blk)],
        out_shape=[jax.ShapeDtypeStruct((m, 2 * width), BF16),
                   jax.ShapeDtypeStruct((m, width), BF16),
                   jax.ShapeDtypeStruct((m, width), BF16),
                   jax.ShapeDtypeStruct((m, width), BF16)],
        scratch_shapes=[pltpu.VMEM((bm, width), F32),
                        pltpu.VMEM((bm, width), BF16),
                        pltpu.VMEM((bm, width), F32),
                        pltpu.VMEM((bm, width), F32),
                        pltpu.VMEM((8, width), F32)],
        compiler_params=_params("arbitrary", "arbitrary"),
        name="mixer_in",
    )(n, w, sgu_norm.reshape(1, width), sgu_w_s, sgu_b.reshape(n_groups, SGU_CHUNK, 1), conv_w)


def _bf16_split3(x):
    p1 = x.astype(BF16).astype(F32)
    r1 = x - p1
    p2 = r1.astype(BF16).astype(F32)
    p3 = (r1 - p2).astype(BF16).astype(F32)
    return p1, p2, p3


def _diff_attn_kernel(scal_ref, q_ref, k_ref, v_ref, lam_ref, sub_ref, o_ref, kt_ref, vx_ref,
                      *, n_heads, tq):
    heads_here = kt_ref.shape[0]
    lam_init = scal_ref[n_heads]
    half = HEAD_DIM // 2
    seq = q_ref.shape[1]

    pos = lax.broadcasted_iota(jnp.int32, (HEAD_DIM, seq), 1)
    frow = lax.broadcasted_iota(jnp.int32, (HEAD_DIM, seq), 0)
    pos_hi = (pos - pos % HEAD_DIM).astype(F32)
    pos_lo = (pos % HEAD_DIM).astype(F32)
    pos_rows = jnp.where(frow < 3, pos_hi, jnp.where(frow < 6, pos_lo, 0.0)).astype(BF16)
    for hh in range(heads_here):
        cols = slice(hh * HEAD_DIM, (hh + 1) * HEAD_DIM)
        kt_ref[hh, :HEAD_DIM, :] = k_ref[0, :, cols].T
        kt_ref[hh, HEAD_DIM:, :] = pos_rows
        vx_ref[hh, :, :HEAD_DIM] = v_ref[0, :, cols]
        vx_ref[hh, :, HEAD_DIM:] = jnp.ones((seq, HEAD_DIM), BF16)

    lp = lam_ref[...]
    lam = (jnp.exp(jnp.sum(lp[0:1] * lp[1:2], axis=-1, keepdims=True))
           - jnp.exp(jnp.sum(lp[2:3] * lp[3:4], axis=-1, keepdims=True)) + lam_init)
    sub = sub_ref[...]

    lane = lax.broadcasted_iota(jnp.int32, (tq, HEAD_DIM), 1)
    flane = lax.broadcasted_iota(jnp.int32, (2 * tq, HEAD_DIM), 1)
    r_idx = lax.broadcasted_iota(jnp.int32, (2 * tq, tq), 0)
    c_idx = lax.broadcasted_iota(jnp.int32, (2 * tq, tq), 1)
    causal = c_idx <= jnp.where(r_idx >= tq, r_idx - tq, r_idx)

    for hh in range(heads_here):
        cols = slice(hh * HEAD_DIM, (hh + 1) * HEAD_DIM)
        slope = scal_ref[pl.program_id(1) * heads_here + hh]
        s1, s2, s3 = _bf16_split3(jnp.full((2 * tq, HEAD_DIM), slope, F32))
        piece = jnp.where(flane % 3 == 0, s1, jnp.where(flane % 3 == 1, s2, s3))
        q_feat = jnp.where(flane < 6, piece, 0.0).astype(BF16)

        for i in reversed(range(seq // tq)):
            start, stop = i * tq, (i + 1) * tq
            q = q_ref[0, start:stop, cols].astype(F32) * (1.0 / math.sqrt(half))
            qq = jnp.concatenate([jnp.where(lane < half, q, 0.0),
                                  jnp.where(lane >= half, q, 0.0)], axis=0).astype(BF16)
            t = _dot(jnp.concatenate([qq, q_feat], axis=1), kt_ref[hh, :, :stop])
            t_diag = jnp.where(causal, t[:, start:], NEG_BIG)
            m = jnp.max(t_diag, axis=-1, keepdims=True)
            if i:
                t_past = t[:, :start]
                m = jnp.maximum(m, jnp.max(t_past, axis=-1, keepdims=True))
                p = jnp.concatenate([jnp.exp(t_past - m), jnp.exp(t_diag - m)], axis=1)
            else:
                p = jnp.exp(t_diag - m)
            a = _dot(p.astype(BF16), vx_ref[hh, :stop, :])
            a = a[:, :HEAD_DIM] / a[:, HEAD_DIM:]
            o = a[:tq] - lam * a[tq:]
            o_ref[0, start:stop, cols] = (_rms_rows(o, sub) * (1.0 - lam_init)).astype(o_ref.dtype)


def diff_attention(qk, v, lam_params, subln, lam_init, *, batch, seq, tq_target=256):
    width = v.shape[1]
    n_heads = width // HEAD_DIM
    tq = _pick(seq, tq_target)
    slopes = 2.0 ** (-8.0 * jnp.arange(1, n_heads + 1, dtype=F32) / n_heads)
    scal = jnp.concatenate([slopes, jnp.full((1,), lam_init, F32)])
    qk3 = qk.reshape(batch, seq, 2 * width)
    v3 = v.reshape(batch, seq, width)
    hps = 2 if n_heads % 2 == 0 else 1
    groups = n_heads // hps
    head_blk = lambda off: pl.BlockSpec((1, seq, hps * HEAD_DIM), lambda b, h: (b, 0, off + h))
    out = pl.pallas_call(
        functools.partial(_diff_attn_kernel, n_heads=n_heads, tq=tq),
        grid=(batch, groups),
        in_specs=[pl.BlockSpec(memory_space=pltpu.SMEM),
                  head_blk(0), head_blk(groups), head_blk(0),
                  pl.BlockSpec(lam_params.shape, lambda b, h: (0, 0)),
                  pl.BlockSpec((1, HEAD_DIM), lambda b, h: (0, 0))],
        out_specs=head_blk(0),
        out_shape=jax.ShapeDtypeStruct((batch, seq, width), BF16),
        scratch_shapes=[pltpu.VMEM((hps, 2 * HEAD_DIM, seq), BF16),
                        pltpu.VMEM((hps, seq, 2 * HEAD_DIM), BF16)],
        compiler_params=_params("parallel", "parallel"),
        name="diff_attention",
    )(scal, qk3, qk3, v3, lam_params, subln.reshape(1, HEAD_DIM))
    return out.reshape(batch * seq, width)


def _merge_kernel(n_ref, wg0_ref, wg1_ref, wg2_ref, ya_ref, yb_ref, yc_ref,
                  wa_ref, wb_ref, wc_ref, o_ref):
    n = n_ref[...]

    def branch(wg_ref, y_ref, w_ref):
        return jax.nn.sigmoid(_dot(n, wg_ref[...])) * _dot(y_ref[...], w_ref[...])

    merged = branch(wg0_ref, ya_ref, wa_ref) + branch(wg1_ref, yb_ref, wb_ref)
    o_ref[...] = (merged + branch(wg2_ref, yc_ref, wc_ref)).astype(o_ref.dtype)


def gated_merge(n, w_mix_in, layer, y_a, y_b, y_c, w_a, w_b, w_c, *, bm_target=1024, bn_target=512):
    m, d = n.shape
    width = y_a.shape[1]
    bm, bn = _pick(m, bm_target), _pick(d, bn_target)
    nb = d // bn
    gate0 = (w_mix_in.shape[2] - N_BRANCHES * d) // bn
    row_blk = lambda i, j: (i, 0)
    col_blk = lambda i, j: (0, j)
    return pl.pallas_call(
        _merge_kernel,
        grid=(m // bm, nb),
        in_specs=[pl.BlockSpec((bm, d), row_blk),
                  _layer_spec((d, bn), layer, lambda i, j: (0, gate0 + j)),
                  _layer_spec((d, bn), layer, lambda i, j: (0, gate0 + nb + j)),
                  _layer_spec((d, bn), layer, lambda i, j: (0, gate0 + 2 * nb + j)),
                  pl.BlockSpec((bm, width), row_blk),
                  pl.BlockSpec((bm, width), row_blk),
                  pl.BlockSpec((bm, width), row_blk),
                  _layer_spec((width, bn), layer, col_blk),
                  _layer_spec((width, bn), layer, col_blk),
                  _layer_spec((width, bn), layer, col_blk)],
        out_specs=pl.BlockSpec((bm, bn), lambda i, j: (i, j)),
        out_shape=jax.ShapeDtypeStruct((m, d), BF16),
        compiler_params=_params("parallel", "arbitrary"),
        name="gated_merge",
    )(n, w_mix_in, w_mix_in, w_mix_in, y_a, y_b, y_c, w_a, w_b, w_c)


def _mem_attn_kernel(x_ref, n_ref, wq_ref, kv_ref, wo_ref, g_next_ref, o_ref, n_next_ref):
    mem_width = wq_ref.shape[1]
    q = _dot(n_ref[...], wq_ref[...]).astype(BF16)
    kv = kv_ref[0]
    scale = 1.0 / math.sqrt(HEAD_DIM)
    heads = []
    for hd in range(mem_width // HEAD_DIM):
        cols = slice(hd * HEAD_DIM, (hd + 1) * HEAD_DIM)
        k_h = kv[:, hd * HEAD_DIM:(hd + 1) * HEAD_DIM]
        v_h = kv[:, mem_width + hd * HEAD_DIM:mem_width + (hd + 1) * HEAD_DIM]
        s = _dot_t(q[:, cols], k_h) * scale
        p = jnp.exp(s - jnp.max(s, axis=-1, keepdims=True))
        denom = jnp.sum(p, axis=-1, keepdims=True)
        heads.append(_dot(p.astype(BF16), v_h) / denom)
    o = jnp.concatenate(heads, axis=1).astype(BF16)
    h = x_ref[...] + _dot(o, wo_ref[...])
    o_ref[...] = h
    n_next_ref[...] = _rms_rows(h, g_next_ref[...]).astype(n_next_ref.dtype)


def memory_attention(x, n, w_q, kv, w_o, layer, gain_next, *, seq, bm_target=512):
    m, d = x.shape
    mem_width = w_q.shape[2]
    bm = _pick(seq, bm_target)
    per_seq = seq // bm
    return pl.pallas_call(
        _mem_attn_kernel,
        grid=(m // bm,),
        in_specs=[pl.BlockSpec((bm, d), lambda i: (i, 0)),
                  pl.BlockSpec((bm, d), lambda i: (i, 0)),
                  _layer_spec((d, mem_width), layer, lambda i: (0, 0)),
                  pl.BlockSpec((1,) + kv.shape[1:], lambda i: (i // per_seq, 0, 0)),
                  _layer_spec((mem_width, d), layer, lambda i: (0, 0)),
                  pl.BlockSpec((1, d), lambda i: (0, 0))],
        out_specs=[pl.BlockSpec((bm, d), lambda i: (i, 0)),
                   pl.BlockSpec((bm, d), lambda i: (i, 0))],
        out_shape=[jax.ShapeDtypeStruct((m, d), F32),
                   jax.ShapeDtypeStruct((m, d), BF16)],
        compiler_params=_params("parallel"),
        name="memory_attention",
    )(x, n, w_q, kv, w_o, gain_next.reshape(1, d))


def kernel(x, mem, ffn1_norm, ffn1_w_in, ffn1_w_out, mix_norm, mix_w_in, diff_lambda, diff_subln, diff_w_out, sgu_norm, sgu_w_s, sgu_b, sgu_w_out, conv_w, conv_w_out, mix_w_o, xattn_norm, mem_norm, xattn_w_q, xattn_w_kv, xattn_w_o, ffn2_norm, ffn2_w_in, ffn2_w_out, final_norm):
    batch, seq, d = x.shape
    n_mem = mem.shape[1]
    depth = ffn1_norm.shape[0]
    h = x.reshape(batch * seq, d)
    mem2 = mem.reshape(batch * n_mem, d)

    bf = lambda w: w.astype(BF16)
    ffn1_w_in, ffn1_w_out, ffn2_w_in, ffn2_w_out = map(bf, (ffn1_w_in, ffn1_w_out, ffn2_w_in, ffn2_w_out))
    mix_w_in, diff_w_out, sgu_w_out, conv_w_out, mix_w_o = map(
        bf, (mix_w_in, diff_w_out, sgu_w_out, conv_w_out, mix_w_o))
    xattn_w_q, xattn_w_kv, xattn_w_o = map(bf, (xattn_w_q, xattn_w_kv, xattn_w_o))

    hidden = norm_swiglu_in(h, ffn1_norm[0], ffn1_w_in, 0)
    for l in range(depth):
        lam_init = 0.8 - 0.6 * math.exp(-0.3 * l)

        h, n_mix = matmul_residual_norm(hidden, ffn1_w_out, l, h, 0.5, mix_norm[l])

        qk, v, y_b, y_c = mixer_in(n_mix, mix_w_in, l, sgu_norm[l], sgu_w_s[l], sgu_b[l], conv_w[l], seq=seq)
        y_a = diff_attention(qk, v, diff_lambda[l], diff_subln[l], lam_init, batch=batch, seq=seq)
        merged = gated_merge(n_mix, mix_w_in, l, y_a, y_b, y_c, diff_w_out, sgu_w_out, conv_w_out)
        h, n_xattn = matmul_residual_norm(merged, mix_w_o, l, h, 1.0, xattn_norm[l], bm_target=512)

        kv = norm_matmul(mem2, mem_norm[l], xattn_w_kv, l).reshape(batch, n_mem, -1)
        h, n_ffn2 = memory_attention(h, n_xattn, xattn_w_q, kv, xattn_w_o, l, ffn2_norm[l], seq=seq)

        hidden = swiglu_in(n_ffn2, ffn2_w_in, l)
        if l + 1 < depth:
            h, n_ffn1 = matmul_residual_norm(hidden, ffn2_w_out, l, h, 0.5, ffn1_norm[l + 1])
            hidden = swiglu_in(n_ffn1, ffn1_w_in, l + 1)

    out = matmul_residual_norm(hidden, ffn2_w_out, depth - 1, h, 0.5, final_norm, final=True)
    return out.reshape(batch, seq, d)
```

```python
import functools
import math

import jax
import jax.numpy as jnp
from jax import lax
from jax.experimental import pallas as pl
from jax.experimental.pallas import tpu as pltpu

HEAD_DIM = 128
SGU_CHUNK = 128
N_MEM_HEADS = 4
N_BRANCHES = 3
CONV_TAPS = 3
RMS_EPS = 1e-6
NEG_BIG = -1e30
V7X_VMEM_LIMIT_BYTES = 60 * 1024 * 1024
NORM_ROWS = 256

BF16 = jnp.bfloat16
F32 = jnp.float32


def _params(*sem):
    return pltpu.CompilerParams(dimension_semantics=sem, vmem_limit_bytes=V7X_VMEM_LIMIT_BYTES)


def _pick(total, target):
    b = min(total, target)
    while total % b:
        b //= 2
    return b


def _rms_rows(x, g):
    ms = jnp.mean(x * x, axis=-1, keepdims=True)
    return x * lax.rsqrt(ms + RMS_EPS) * g


def _fill_norm(x_ref, g_ref, n_ref):
    bm = x_ref.shape[0]
    rows = min(NORM_ROWS, bm)
    g = g_ref[...]

    def body(c, carry):
        r0 = pl.multiple_of(c * rows, rows)
        n_ref[pl.ds(r0, rows), :] = _rms_rows(x_ref[pl.ds(r0, rows), :], g).astype(n_ref.dtype)
        return carry

    lax.fori_loop(0, bm // rows, body, 0)


def _dot(a, b):
    return jnp.dot(a, b, preferred_element_type=F32)


def _dot_t(a, b):
    return lax.dot_general(a, b, (((1,), (1,)), ((), ())), preferred_element_type=F32)


def _gelu(x):
    return 0.5 * x * (1.0 + lax.erf(x * (1.0 / math.sqrt(2.0))))


def _norm_mm_kernel(x_ref, g_ref, w_ref, o_ref, n_ref):
    @pl.when(pl.program_id(1) == 0)
    def _():
        _fill_norm(x_ref, g_ref, n_ref)

    o_ref[...] = _dot(n_ref[...], w_ref[...]).astype(o_ref.dtype)


def _layer_spec(block, layer, index_map, *, single_buffer=False):
    return pl.BlockSpec((None,) + block, lambda *g: (layer,) + index_map(*g),
                        pipeline_mode=pl.Buffered(1) if single_buffer else None)


def norm_matmul(x, gain, w, layer, *, bm_target=1024, bn_target=1024):
    m, d = x.shape
    n = w.shape[2]
    bm, bn = _pick(m, bm_target), _pick(n, bn_target)
    return pl.pallas_call(
        _norm_mm_kernel,
        grid=(m // bm, n // bn),
        in_specs=[pl.BlockSpec((bm, d), lambda i, j: (i, 0)),
                  pl.BlockSpec((1, d), lambda i, j: (0, 0)),
                  _layer_spec((d, bn), layer, lambda i, j: (0, j))],
        out_specs=pl.BlockSpec((bm, bn), lambda i, j: (i, j)),
        out_shape=jax.ShapeDtypeStruct((m, n), BF16),
        scratch_shapes=[pltpu.VMEM((bm, d), BF16)],
        compiler_params=_params("parallel", "arbitrary"),
        name="norm_matmul",
    )(x, gain.reshape(1, d), w)


def _norm_swiglu_kernel(x_ref, g_ref, wa_ref, wb_ref, o_ref, n_ref):
    @pl.when(pl.program_id(1) == 0)
    def _():
        _fill_norm(x_ref, g_ref, n_ref)

    n = n_ref[...]
    a = _dot(n, wa_ref[...])
    b = _dot(n, wb_ref[...])
    o_ref[...] = (a * jax.nn.sigmoid(a) * b).astype(o_ref.dtype)


def norm_swiglu_in(x, gain, w_in, layer, *, bm_target=1024, bn_target=512):
    m, d = x.shape
    f = w_in.shape[2] // 2
    bm, bn = _pick(m, bm_target), _pick(f, bn_target)
    nb = f // bn
    return pl.pallas_call(
        _norm_swiglu_kernel,
        grid=(m // bm, nb),
        in_specs=[pl.BlockSpec((bm, d), lambda i, j: (i, 0)),
                  pl.BlockSpec((1, d), lambda i, j: (0, 0)),
                  _layer_spec((d, bn), layer, lambda i, j: (0, j)),
                  _layer_spec((d, bn), layer, lambda i, j: (0, j + nb))],
        out_specs=pl.BlockSpec((bm, bn), lambda i, j: (i, j)),
        out_shape=jax.ShapeDtypeStruct((m, f), BF16),
        scratch_shapes=[pltpu.VMEM((bm, d), BF16)],
        compiler_params=_params("parallel", "arbitrary"),
        name="norm_swiglu_in",
    )(x, gain.reshape(1, d), w_in, w_in)


def _swiglu_kernel(n_ref, wa_ref, wb_ref, o_ref):
    n = n_ref[...]
    a = _dot(n, wa_ref[...])
    b = _dot(n, wb_ref[...])
    o_ref[...] = (a * jax.nn.sigmoid(a) * b).astype(o_ref.dtype)


def swiglu_in(n, w_in, layer, *, bm_target=1024, bn_target=512):
    m, d = n.shape
    f = w_in.shape[2] // 2
    bm, bn = _pick(m, bm_target), _pick(f, bn_target)
    nb = f // bn
    return pl.pallas_call(
        _swiglu_kernel,
        grid=(m // bm, nb),
        in_specs=[pl.BlockSpec((bm, d), lambda i, j: (i, 0)),
                  _layer_spec((d, bn), layer, lambda i, j: (0, j)),
                  _layer_spec((d, bn), layer, lambda i, j: (0, j + nb))],
        out_specs=pl.BlockSpec((bm, bn), lambda i, j: (i, j)),
        out_shape=jax.ShapeDtypeStruct((m, f), BF16),
        compiler_params=_params("parallel", "parallel"),
        name="swiglu_in",
    )(n, w_in, w_in)


def _mm_res_norm_kernel(x_ref, w_ref, r_ref, g_ref, *out_refs, scale, final):
    h = r_ref[...] + scale * _dot(x_ref[...], w_ref[...])
    normed = _rms_rows(h, g_ref[...])
    if final:
        out_refs[0][...] = normed
    else:
        out_refs[0][...] = h
        out_refs[1][...] = normed.astype(out_refs[1].dtype)


def matmul_residual_norm(x, w, layer, res, scale, gain, *, final=False, bm_target=256):
    m, k = x.shape
    n = w.shape[2]
    bm = _pick(m, bm_target)
    row_blk = lambda i: (i, 0)
    out_specs = [pl.BlockSpec((bm, n), row_blk)]
    out_shape = [jax.ShapeDtypeStruct((m, n), F32)]
    if not final:
        out_specs.append(pl.BlockSpec((bm, n), row_blk))
        out_shape.append(jax.ShapeDtypeStruct((m, n), BF16))
    out = pl.pallas_call(
        functools.partial(_mm_res_norm_kernel, scale=scale, final=final),
        grid=(m // bm,),
        in_specs=[pl.BlockSpec((bm, k), row_blk),
                  _layer_spec((k, n), layer, lambda i: (0, 0), single_buffer=True),
                  pl.BlockSpec((bm, n), row_blk),
                  pl.BlockSpec((1, n), lambda i: (0, 0))],
        out_specs=out_specs,
        out_shape=out_shape,
        compiler_params=_params("parallel"),
        name="matmul_residual_norm",
    )(x, w, res, gain.reshape(1, n))
    return out[0] if final else out


MIX_PAIRS = 4
MIX_FIRST_PAIR = 1
(STEP_VU, STEP_GB, STEP_CX, STEP_QK) = range(MIX_PAIRS)


def _mix_in_kernel(n_ref, w_ref, sgun_ref, ws_ref, sb_ref, cw_ref,
                   qk_ref, v_ref, yb_ref, yc_ref,
                   u_ref, gv_ref, cb_ref, z_ref, carry_ref, *, blocks_per_seq):
    i, j = pl.program_id(0), pl.program_id(1)
    bm, width = u_ref.shape
    n_groups = width // HEAD_DIM
    n_chunks = bm // SGU_CHUNK
    first, second = slice(0, width), slice(width, 2 * width)

    def project(cols=slice(None)):
        return _dot(n_ref[...], w_ref[:, cols])

    @pl.when(j == STEP_VU)
    def _():
        v_ref[...] = project(first).astype(v_ref.dtype)
        u_ref[...] = _gelu(project(second))

    @pl.when(j == STEP_GB)
    def _():
        gv_ref[...] = _rms_rows(_gelu(project(first)), sgun_ref[...]).astype(gv_ref.dtype)
        cb_ref[...] = project(second)

    @pl.when(j == STEP_CX)
    def _():
        t_idx = lax.broadcasted_iota(jnp.int32, (SGU_CHUNK, SGU_CHUNK), 0)
        s_idx = lax.broadcasted_iota(jnp.int32, (SGU_CHUNK, SGU_CHUNK), 1)

        def sgu_groups(groups):
            for g in groups:
                cols = slice(g * HEAD_DIM, (g + 1) * HEAD_DIM)
                w_tril = jnp.where(s_idx <= t_idx, ws_ref[g], 0.0).astype(BF16)
                xg = jnp.concatenate(
                    [gv_ref[c * SGU_CHUNK:(c + 1) * SGU_CHUNK, cols] for c in range(n_chunks)], axis=1)
                mixed = _dot(w_tril, xg) + sb_ref[g]
                for c in range(n_chunks):
                    rows = slice(c * SGU_CHUNK, (c + 1) * SGU_CHUNK)
                    yb_ref[rows, cols] = (
                        u_ref[rows, cols] * mixed[:, c * HEAD_DIM:(c + 1) * HEAD_DIM]).astype(yb_ref.dtype)

        z_ref[...] = project(first)
        sgu_groups(range(n_groups // 2))
        z_ref[...] = z_ref[...] * project(second)
        sgu_groups(range(n_groups // 2, n_groups))

    @pl.when((j == STEP_CX) & (i % blocks_per_seq == 0))
    def _():
        carry_ref[...] = jnp.zeros_like(carry_ref)

    @pl.when(j == STEP_QK)
    def _():
        qk_ref[...] = project().astype(qk_ref.dtype)

    @pl.when(j > STEP_CX)
    def _():
        z = z_ref[...]
        prev1 = carry_ref[7:8, :]
        prev2 = carry_ref[6:7, :]
        row = lax.broadcasted_iota(jnp.int32, z.shape, 0)
        z1 = jnp.where(row == 0, prev1, pltpu.roll(z, 1, 0))
        z2 = jnp.where(row == 0, prev2, jnp.where(row == 1, prev1, pltpu.roll(z, 2, 0)))
        conv = cw_ref[0:1, :] * z2 + cw_ref[1:2, :] * z1 + cw_ref[2:3, :] * z
        yc_ref[...] = (cb_ref[...] * conv).astype(yc_ref.dtype)
        carry_ref[...] = z[bm - 8:bm, :]


def mixer_in(n, w, layer, sgu_norm, sgu_w_s, sgu_b, conv_w, *, seq, bm_target=512):
    m, d = n.shape
    width = d // 2
    assert w.shape[1] == d and w.shape[2] >= 2 * MIX_PAIRS * width
    bm = _pick(seq, bm_target)
    assert bm % SGU_CHUNK == 0 and bm >= 8
    n_groups = width // HEAD_DIM
    const2 = lambda i, j: (0, 0)
    const3 = lambda i, j: (0, 0, 0)
    row_blk = lambda i, j: (i, 0)
    return pl.pallas_call(
        functools.partial(_mix_in_kernel, blocks_per_seq=seq // bm),
        grid=(m // bm, MIX_PAIRS),
        in_specs=[pl.BlockSpec((bm, d), row_blk),
                  _layer_spec((d, 2 * width), layer, lambda i, j: (0, (j + MIX_FIRST_PAIR) % MIX_PAIRS)),
                  pl.BlockSpec((1, width), const2),
                  pl.BlockSpec((n_groups, SGU_CHUNK, SGU_CHUNK), const3),
                  pl.BlockSpec((n_groups, SGU_CHUNK, 1), const3),
                  pl.BlockSpec((CONV_TAPS, width), const2)],
        out_specs=[pl.BlockSpec((bm, 2 * width), row_blk),
                   pl.BlockSpec((bm, width), row_blk),
                   pl.BlockSpec((bm, width), row_blk),
                   pl.BlockSpec((bm, width), row_blk)],
        out_shape=[jax.ShapeDtypeStruct((m, 2 * width), BF16),
                   jax.ShapeDtypeStruct((m, width), BF16),
                   jax.ShapeDtypeStruct((m, width), BF16),
                   jax.ShapeDtypeStruct((m, width), BF16)],
        scratch_shapes=[pltpu.VMEM((bm, width), F32),
                        pltpu.VMEM((bm, width), BF16),
                        pltpu.VMEM((bm, width), F32),
                        pltpu.VMEM((bm, width), F32),
                        pltpu.VMEM((8, width), F32)],
        compiler_params=_params("arbitrary", "arbitrary"),
        name="mixer_in",
    )(n, w, sgu_norm.reshape(1, width), sgu_w_s, sgu_b.reshape(n_groups, SGU_CHUNK, 1), conv_w)


def _bf16_split3(x):
    p1 = x.astype(BF16).astype(F32)
    r1 = x - p1
    p2 = r1.astype(BF16).astype(F32)
    p3 = (r1 - p2).astype(BF16).astype(F32)
    return p1, p2, p3


CAST_LANES = 1024


def _diff_attn_kernel(scal_ref, q_ref, k_ref, v_ref, lam_ref, sub_ref, *rest, n_heads, tq, n_cast):
    cast_in, o_ref, cast_out = rest[:n_cast], rest[n_cast], rest[n_cast + 1:2 * n_cast + 1]
    kt_ref, vx_ref = rest[2 * n_cast + 1:]
    for src, dst in zip(cast_in, cast_out):
        dst[...] = src[...].astype(dst.dtype)

    heads_here = kt_ref.shape[0]
    lam_init = scal_ref[n_heads]
    half = HEAD_DIM // 2
    seq = q_ref.shape[1]

    pos = lax.broadcasted_iota(jnp.int32, (HEAD_DIM, seq), 1)
    frow = lax.broadcasted_iota(jnp.int32, (HEAD_DIM, seq), 0)
    pos_hi = (pos - pos % HEAD_DIM).astype(F32)
    pos_lo = (pos % HEAD_DIM).astype(F32)
    pos_rows = jnp.where(frow < 3, pos_hi, jnp.where(frow < 6, pos_lo, 0.0)).astype(BF16)
    for hh in range(heads_here):
        cols = slice(hh * HEAD_DIM, (hh + 1) * HEAD_DIM)
        kt_ref[hh, :HEAD_DIM, :] = k_ref[0, :, cols].T
        kt_ref[hh, HEAD_DIM:, :] = pos_rows
        vx_ref[hh, :, :HEAD_DIM] = v_ref[0, :, cols]
        vx_ref[hh, :, HEAD_DIM:] = jnp.ones((seq, HEAD_DIM), BF16)

    lp = lam_ref[...]
    lam = (jnp.exp(jnp.sum(lp[0:1] * lp[1:2], axis=-1, keepdims=True))
           - jnp.exp(jnp.sum(lp[2:3] * lp[3:4], axis=-1, keepdims=True)) + lam_init)
    sub = sub_ref[...]

    lane = lax.broadcasted_iota(jnp.int32, (tq, HEAD_DIM), 1)
    flane = lax.broadcasted_iota(jnp.int32, (2 * tq, HEAD_DIM), 1)
    r_idx = lax.broadcasted_iota(jnp.int32, (2 * tq, tq), 0)
    c_idx = lax.broadcasted_iota(jnp.int32, (2 * tq, tq), 1)
    causal = c_idx <= jnp.where(r_idx >= tq, r_idx - tq, r_idx)

    for hh in range(heads_here):
        cols = slice(hh * HEAD_DIM, (hh + 1) * HEAD_DIM)
        slope = scal_ref[pl.program_id(1) * heads_here + hh]
        s1, s2, s3 = _bf16_split3(jnp.full((2 * tq, HEAD_DIM), slope, F32))
        piece = jnp.where(flane % 3 == 0, s1, jnp.where(flane % 3 == 1, s2, s3))
        q_feat = jnp.where(flane < 6, piece, 0.0).astype(BF16)

        for i in reversed(range(seq // tq)):
            start, stop = i * tq, (i + 1) * tq
            q = q_ref[0, start:stop, cols].astype(F32) * (1.0 / math.sqrt(half))
            qq = jnp.concatenate([jnp.where(lane < half, q, 0.0),
                                  jnp.where(lane >= half, q, 0.0)], axis=0).astype(BF16)
            t = _dot(jnp.concatenate([qq, q_feat], axis=1), kt_ref[hh, :, :stop])
            t_diag = jnp.where(causal, t[:, start:], NEG_BIG)
            m = jnp.max(t_diag, axis=-1, keepdims=True)
            if i:
                t_past = t[:, :start]
                m = jnp.maximum(m, jnp.max(t_past, axis=-1, keepdims=True))
                p = jnp.concatenate([jnp.exp(t_past - m), jnp.exp(t_diag - m)], axis=1)
            else:
                p = jnp.exp(t_diag - m)
            a = _dot(p.astype(BF16), vx_ref[hh, :stop, :])
            a = a[:, :HEAD_DIM] / a[:, HEAD_DIM:]
            o = a[:tq] - lam * a[tq:]
            o_ref[0, start:stop, cols] = (_rms_rows(o, sub) * (1.0 - lam_init)).astype(o_ref.dtype)


def diff_attention(qk, v, lam_params, subln, lam_init, cast_stacks=(), cast_layer=0, *,
                   batch, seq, tq_target=256):
    width = v.shape[1]
    n_heads = width // HEAD_DIM
    tq = _pick(seq, tq_target)
    slopes = 2.0 ** (-8.0 * jnp.arange(1, n_heads + 1, dtype=F32) / n_heads)
    scal = jnp.concatenate([slopes, jnp.full((1,), lam_init, F32)])
    qk3 = qk.reshape(batch, seq, 2 * width)
    v3 = v.reshape(batch, seq, width)
    hps = _pick(n_heads, 2)
    groups = n_heads // hps
    head_blk = lambda off: pl.BlockSpec((1, seq, hps * HEAD_DIM), lambda b, h: (b, 0, off + h))

    steps = batch * groups
    flat, cast_in_specs, cast_out_specs, cast_out_shapes = [], [], [], []
    for w in cast_stacks:
        rows = w.shape[1] * w.shape[2] // CAST_LANES
        per_step = rows // steps
        assert rows * CAST_LANES == w.shape[1] * w.shape[2] and per_step * steps == rows and per_step % 16 == 0
        flat.append(w.reshape(w.shape[0], rows, CAST_LANES))
        cast_in_specs.append(pl.BlockSpec((None, per_step, CAST_LANES),
                                          lambda b, h: (cast_layer, b * groups + h, 0)))
        cast_out_specs.append(pl.BlockSpec((per_step, CAST_LANES), lambda b, h: (b * groups + h, 0)))
        cast_out_shapes.append(jax.ShapeDtypeStruct((rows, CAST_LANES), BF16))

    out = pl.pallas_call(
        functools.partial(_diff_attn_kernel, n_heads=n_heads, tq=tq, n_cast=len(flat)),
        grid=(batch, groups),
        in_specs=[pl.BlockSpec(memory_space=pltpu.SMEM),
                  head_blk(0), head_blk(groups), head_blk(0),
                  pl.BlockSpec(lam_params.shape, lambda b, h: (0, 0)),
                  pl.BlockSpec((1, HEAD_DIM), lambda b, h: (0, 0))] + cast_in_specs,
        out_specs=[head_blk(0)] + cast_out_specs,
        out_shape=[jax.ShapeDtypeStruct((batch, seq, width), BF16)] + cast_out_shapes,
        scratch_shapes=[pltpu.VMEM((hps, 2 * HEAD_DIM, seq), BF16),
                        pltpu.VMEM((hps, seq, 2 * HEAD_DIM), BF16)],
        compiler_params=_params("parallel", "parallel"),
        name="diff_attention",
    )(scal, qk3, qk3, v3, lam_params, subln.reshape(1, HEAD_DIM), *flat)
    casted = [c.reshape((1,) + w.shape[1:]) for c, w in zip(out[1:], cast_stacks)]
    return out[0].reshape(batch * seq, width), casted


def _merge_kernel(n_ref, wg0_ref, wg1_ref, wg2_ref, ya_ref, yb_ref, yc_ref,
                  wa_ref, wb_ref, wc_ref, o_ref):
    n = n_ref[...]

    def branch(wg_ref, y_ref, w_ref):
        return jax.nn.sigmoid(_dot(n, wg_ref[...])) * _dot(y_ref[...], w_ref[...])

    merged = branch(wg0_ref, ya_ref, wa_ref) + branch(wg1_ref, yb_ref, wb_ref)
    o_ref[...] = (merged + branch(wg2_ref, yc_ref, wc_ref)).astype(o_ref.dtype)


def gated_merge(n, w_mix_in, layer, y_a, y_b, y_c, w_a, w_b, w_c, *, bm_target=1024, bn_target=512):
    m, d = n.shape
    width = y_a.shape[1]
    bm, bn = _pick(m, bm_target), _pick(d, bn_target)
    nb = d // bn
    gate0 = (w_mix_in.shape[2] - N_BRANCHES * d) // bn
    row_blk = lambda i, j: (i, 0)
    col_blk = lambda i, j: (0, j)
    return pl.pallas_call(
        _merge_kernel,
        grid=(m // bm, nb),
        in_specs=[pl.BlockSpec((bm, d), row_blk),
                  _layer_spec((d, bn), layer, lambda i, j: (0, gate0 + j)),
                  _layer_spec((d, bn), layer, lambda i, j: (0, gate0 + nb + j)),
                  _layer_spec((d, bn), layer, lambda i, j: (0, gate0 + 2 * nb + j)),
                  pl.BlockSpec((bm, width), row_blk),
                  pl.BlockSpec((bm, width), row_blk),
                  pl.BlockSpec((bm, width), row_blk),
                  _layer_spec((width, bn), layer, col_blk),
                  _layer_spec((width, bn), layer, col_blk),
                  _layer_spec((width, bn), layer, col_blk)],
        out_specs=pl.BlockSpec((bm, bn), lambda i, j: (i, j)),
        out_shape=jax.ShapeDtypeStruct((m, d), BF16),
        compiler_params=_params("parallel", "arbitrary"),
        name="gated_merge",
    )(n, w_mix_in, w_mix_in, w_mix_in, y_a, y_b, y_c, w_a, w_b, w_c)


def _mem_attn_kernel(x_ref, n_ref, wq_ref, kv_ref, wo_ref, g_next_ref, o_ref, n_next_ref):
    mem_width = wq_ref.shape[1]
    q = _dot(n_ref[...], wq_ref[...]).astype(BF16)
    kv = kv_ref[0]
    scale = 1.0 / math.sqrt(HEAD_DIM)
    heads = []
    for hd in range(mem_width // HEAD_DIM):
        cols = slice(hd * HEAD_DIM, (hd + 1) * HEAD_DIM)
        k_h = kv[:, hd * HEAD_DIM:(hd + 1) * HEAD_DIM]
        v_h = kv[:, mem_width + hd * HEAD_DIM:mem_width + (hd + 1) * HEAD_DIM]
        s = _dot_t(q[:, cols], k_h) * scale
        p = jnp.exp(s - jnp.max(s, axis=-1, keepdims=True))
        denom = jnp.sum(p, axis=-1, keepdims=True)
        heads.append(_dot(p.astype(BF16), v_h) / denom)
    o = jnp.concatenate(heads, axis=1).astype(BF16)
    h = x_ref[...] + _dot(o, wo_ref[...])
    o_ref[...] = h
    n_next_ref[...] = _rms_rows(h, g_next_ref[...]).astype(n_next_ref.dtype)


def memory_attention(x, n, w_q, kv, w_o, layer, gain_next, *, seq, bm_target=512):
    m, d = x.shape
    mem_width = w_q.shape[2]
    bm = _pick(seq, bm_target)
    per_seq = seq // bm
    return pl.pallas_call(
        _mem_attn_kernel,
        grid=(m // bm,),
        in_specs=[pl.BlockSpec((bm, d), lambda i: (i, 0)),
                  pl.BlockSpec((bm, d), lambda i: (i, 0)),
                  _layer_spec((d, mem_width), layer, lambda i: (0, 0)),
                  pl.BlockSpec((1,) + kv.shape[1:], lambda i: (i // per_seq, 0, 0)),
                  _layer_spec((mem_width, d), layer, lambda i: (0, 0)),
                  pl.BlockSpec((1, d), lambda i: (0, 0))],
        out_specs=[pl.BlockSpec((bm, d), lambda i: (i, 0)),
                   pl.BlockSpec((bm, d), lambda i: (i, 0))],
        out_shape=[jax.ShapeDtypeStruct((m, d), F32),
                   jax.ShapeDtypeStruct((m, d), BF16)],
        compiler_params=_params("parallel"),
        name="memory_attention",
    )(x, n, w_q, kv, w_o, gain_next.reshape(1, d))


def kernel(x, mem, ffn1_norm, ffn1_w_in, ffn1_w_out, mix_norm, mix_w_in, diff_lambda, diff_subln, diff_w_out, sgu_norm, sgu_w_s, sgu_b, sgu_w_out, conv_w, conv_w_out, mix_w_o, xattn_norm, mem_norm, xattn_w_q, xattn_w_kv, xattn_w_o, ffn2_norm, ffn2_w_in, ffn2_w_out, final_norm):
    batch, seq, d = x.shape
    n_mem = mem.shape[1]
    depth = ffn1_norm.shape[0]
    h = x.reshape(batch * seq, d)
    mem2 = mem.reshape(batch * n_mem, d)

    big = dict(ffn1_w_in=ffn1_w_in, ffn1_w_out=ffn1_w_out, ffn2_w_in=ffn2_w_in, ffn2_w_out=ffn2_w_out,
               mix_w_in=mix_w_in, diff_w_out=diff_w_out, sgu_w_out=sgu_w_out, conv_w_out=conv_w_out,
               mix_w_o=mix_w_o)
    w = {name: stack[0:1].astype(BF16) for name, stack in big.items()}
    xattn_w_q, xattn_w_kv, xattn_w_o = (s.astype(BF16) for s in (xattn_w_q, xattn_w_kv, xattn_w_o))

    hidden = norm_swiglu_in(h, ffn1_norm[0], w["ffn1_w_in"], 0)
    for l in range(depth):
        lam_init = 0.8 - 0.6 * math.exp(-0.3 * l)
        last = l + 1 == depth

        h, n_mix = matmul_residual_norm(hidden, w["ffn1_w_out"], 0, h, 0.5, mix_norm[l])

        qk, v, y_b, y_c = mixer_in(n_mix, w["mix_w_in"], 0, sgu_norm[l], sgu_w_s[l], sgu_b[l], conv_w[l],
                                   seq=seq)
        y_a, casted = diff_attention(qk, v, diff_lambda[l], diff_subln[l], lam_init,
                                     () if last else tuple(big.values()), l + 1, batch=batch, seq=seq)
        merged = gated_merge(n_mix, w["mix_w_in"], 0, y_a, y_b, y_c,
                             w["diff_w_out"], w["sgu_w_out"], w["conv_w_out"])
        h, n_xattn = matmul_residual_norm(merged, w["mix_w_o"], 0, h, 1.0, xattn_norm[l], bm_target=512)

        kv = norm_matmul(mem2, mem_norm[l], xattn_w_kv, l).reshape(batch, n_mem, -1)
        h, n_ffn2 = memory_attention(h, n_xattn, xattn_w_q, kv, xattn_w_o, l, ffn2_norm[l], seq=seq)

        hidden = swiglu_in(n_ffn2, w["ffn2_w_in"], 0)
        if last:
            out = matmul_residual_norm(hidden, w["ffn2_w_out"], 0, h, 0.5, final_norm, final=True)
        else:
            h, n_ffn1 = matmul_residual_norm(hidden, w["ffn2_w_out"], 0, h, 0.5, ffn1_norm[l + 1])
            w = dict(zip(big, casted))
            hidden = swiglu_in(n_ffn1, w["ffn1_w_in"], 0)

    return out.reshape(batch, seq, d)
```

```python
import functools
import math

import jax
import jax.numpy as jnp
from jax import lax
from jax.experimental import pallas as pl
from jax.experimental.pallas import tpu as pltpu

HEAD_DIM = 128
SGU_CHUNK = 128
N_MEM_HEADS = 4
N_BRANCHES = 3
CONV_TAPS = 3
RMS_EPS = 1e-6
NEG_BIG = -1e30
V7X_VMEM_LIMIT_BYTES = 60 * 1024 * 1024
NORM_ROWS = 256

BF16 = jnp.bfloat16
F32 = jnp.float32


def _params(*sem):
    return pltpu.CompilerParams(dimension_semantics=sem, vmem_limit_bytes=V7X_VMEM_LIMIT_BYTES)


def _pick(total, target):
    b = min(total, target)
    while total % b:
        b //= 2
    return b


def _rms_rows(x, g):
    ms = jnp.mean(x * x, axis=-1, keepdims=True)
    return x * lax.rsqrt(ms + RMS_EPS) * g


def _fill_norm(x_ref, g_ref, n_ref):
    bm = x_ref.shape[0]
    rows = min(NORM_ROWS, bm)
    g = g_ref[...]

    def body(c, carry):
        r0 = pl.multiple_of(c * rows, rows)
        n_ref[pl.ds(r0, rows), :] = _rms_rows(x_ref[pl.ds(r0, rows), :], g).astype(n_ref.dtype)
        return carry

    lax.fori_loop(0, bm // rows, body, 0)


def _dot(a, b):
    return jnp.dot(a, b, preferred_element_type=F32)


def _dot_t(a, b):
    return lax.dot_general(a, b, (((1,), (1,)), ((), ())), preferred_element_type=F32)


def _gelu(x):
    return 0.5 * x * (1.0 + lax.erf(x * (1.0 / math.sqrt(2.0))))


def _norm_mm_kernel(x_ref, g_ref, w_ref, o_ref, n_ref):
    @pl.when(pl.program_id(1) == 0)
    def _():
        _fill_norm(x_ref, g_ref, n_ref)

    o_ref[...] = _dot(n_ref[...], w_ref[...]).astype(o_ref.dtype)


def _layer_spec(block, layer, index_map, *, single_buffer=False):
    return pl.BlockSpec((None,) + block, lambda *g: (layer,) + index_map(*g),
                        pipeline_mode=pl.Buffered(1) if single_buffer else None)


def norm_matmul(x, gain, w, layer, *, bm_target=1024, bn_target=1024):
    m, d = x.shape
    n = w.shape[2]
    bm, bn = _pick(m, bm_target), _pick(n, bn_target)
    return pl.pallas_call(
        _norm_mm_kernel,
        grid=(m // bm, n // bn),
        in_specs=[pl.BlockSpec((bm, d), lambda i, j: (i, 0)),
                  pl.BlockSpec((1, d), lambda i, j: (0, 0)),
                  _layer_spec((d, bn), layer, lambda i, j: (0, j))],
        out_specs=pl.BlockSpec((bm, bn), lambda i, j: (i, j)),
        out_shape=jax.ShapeDtypeStruct((m, n), BF16),
        scratch_shapes=[pltpu.VMEM((bm, d), BF16)],
        compiler_params=_params("parallel", "arbitrary"),
        name="norm_matmul",
    )(x, gain.reshape(1, d), w)


def _norm_swiglu_kernel(x_ref, g_ref, wa_ref, wb_ref, o_ref, n_ref):
    @pl.when(pl.program_id(1) == 0)
    def _():
        _fill_norm(x_ref, g_ref, n_ref)

    n = n_ref[...]
    a = _dot(n, wa_ref[...])
    b = _dot(n, wb_ref[...])
    o_ref[...] = (a * jax.nn.sigmoid(a) * b).astype(o_ref.dtype)


def norm_swiglu_in(x, gain, w_in, layer, *, bm_target=1024, bn_target=512):
    m, d = x.shape
    f = w_in.shape[2] // 2
    bm, bn = _pick(m, bm_target), _pick(f, bn_target)
    nb = f // bn
    return pl.pallas_call(
        _norm_swiglu_kernel,
        grid=(m // bm, nb),
        in_specs=[pl.BlockSpec((bm, d), lambda i, j: (i, 0)),
                  pl.BlockSpec((1, d), lambda i, j: (0, 0)),
                  _layer_spec((d, bn), layer, lambda i, j: (0, j)),
                  _layer_spec((d, bn), layer, lambda i, j: (0, j + nb))],
        out_specs=pl.BlockSpec((bm, bn), lambda i, j: (i, j)),
        out_shape=jax.ShapeDtypeStruct((m, f), BF16),
        scratch_shapes=[pltpu.VMEM((bm, d), BF16)],
        compiler_params=_params("parallel", "arbitrary"),
        name="norm_swiglu_in",
    )(x, gain.reshape(1, d), w_in, w_in)


def _swiglu_kernel(n_ref, wa_ref, wb_ref, o_ref):
    n = n_ref[...]
    a = _dot(n, wa_ref[...])
    b = _dot(n, wb_ref[...])
    o_ref[...] = (a * jax.nn.sigmoid(a) * b).astype(o_ref.dtype)


def swiglu_in(n, w_in, layer, *, bm_target=1024, bn_target=512):
    m, d = n.shape
    f = w_in.shape[2] // 2
    bm, bn = _pick(m, bm_target), _pick(f, bn_target)
    nb = f // bn
    return pl.pallas_call(
        _swiglu_kernel,
        grid=(m // bm, nb),
        in_specs=[pl.BlockSpec((bm, d), lambda i, j: (i, 0)),
                  _layer_spec((d, bn), layer, lambda i, j: (0, j)),
                  _layer_spec((d, bn), layer, lambda i, j: (0, j + nb))],
        out_specs=pl.BlockSpec((bm, bn), lambda i, j: (i, j)),
        out_shape=jax.ShapeDtypeStruct((m, f), BF16),
        compiler_params=_params("parallel", "parallel"),
        name="swiglu_in",
    )(n, w_in, w_in)


def _mm_res_norm_kernel(x_ref, w_ref, r_ref, g_ref, *out_refs, scale, final):
    h = r_ref[...] + scale * _dot(x_ref[...], w_ref[...])
    normed = _rms_rows(h, g_ref[...])
    if final:
        out_refs[0][...] = normed
    else:
        out_refs[0][...] = h
        out_refs[1][...] = normed.astype(out_refs[1].dtype)


def matmul_residual_norm(x, w, layer, res, scale, gain, *, final=False, bm_target=256):
    m, k = x.shape
    n = w.shape[2]
    bm = _pick(m, bm_target)
    row_blk = lambda i: (i, 0)
    out_specs = [pl.BlockSpec((bm, n), row_blk)]
    out_shape = [jax.ShapeDtypeStruct((m, n), F32)]
    if not final:
        out_specs.append(pl.BlockSpec((bm, n), row_blk))
        out_shape.append(jax.ShapeDtypeStruct((m, n), BF16))
    out = pl.pallas_call(
        functools.partial(_mm_res_norm_kernel, scale=scale, final=final),
        grid=(m // bm,),
        in_specs=[pl.BlockSpec((bm, k), row_blk),
                  _layer_spec((k, n), layer, lambda i: (0, 0), single_buffer=True),
                  pl.BlockSpec((bm, n), row_blk),
                  pl.BlockSpec((1, n), lambda i: (0, 0))],
        out_specs=out_specs,
        out_shape=out_shape,
        compiler_params=_params("parallel"),
        name="matmul_residual_norm",
    )(x, w, res, gain.reshape(1, n))
    return out[0] if final else out


MIX_PAIRS = 4
MIX_FIRST_PAIR = 1
(STEP_VU, STEP_GB, STEP_CX, STEP_QK) = range(MIX_PAIRS)


def _mix_in_kernel(n_ref, w_ref, sgun_ref, ws_ref, sb_ref, cw_ref,
                   qk_ref, v_ref, yb_ref, yc_ref,
                   u_ref, gv_ref, cb_ref, z_ref, carry_ref, *, blocks_per_seq):
    i, j = pl.program_id(0), pl.program_id(1)
    bm, width = u_ref.shape
    n_groups = width // HEAD_DIM
    n_chunks = bm // SGU_CHUNK
    first, second = slice(0, width), slice(width, 2 * width)

    def project(cols=slice(None)):
        return _dot(n_ref[...], w_ref[:, cols])

    @pl.when(j == STEP_VU)
    def _():
        v_ref[...] = project(first).astype(v_ref.dtype)
        u_ref[...] = _gelu(project(second))

    @pl.when(j == STEP_GB)
    def _():
        gv_ref[...] = _rms_rows(_gelu(project(first)), sgun_ref[...]).astype(gv_ref.dtype)
        cb_ref[...] = project(second)

    @pl.when(j == STEP_CX)
    def _():
        t_idx = lax.broadcasted_iota(jnp.int32, (SGU_CHUNK, SGU_CHUNK), 0)
        s_idx = lax.broadcasted_iota(jnp.int32, (SGU_CHUNK, SGU_CHUNK), 1)

        def sgu_groups(groups):
            for g in groups:
                cols = slice(g * HEAD_DIM, (g + 1) * HEAD_DIM)
                w_tril = jnp.where(s_idx <= t_idx, ws_ref[g], 0.0).astype(BF16)
                xg = jnp.concatenate(
                    [gv_ref[c * SGU_CHUNK:(c + 1) * SGU_CHUNK, cols] for c in range(n_chunks)], axis=1)
                mixed = _dot(w_tril, xg) + sb_ref[g]
                for c in range(n_chunks):
                    rows = slice(c * SGU_CHUNK, (c + 1) * SGU_CHUNK)
                    yb_ref[rows, cols] = (
                        u_ref[rows, cols] * mixed[:, c * HEAD_DIM:(c + 1) * HEAD_DIM]).astype(yb_ref.dtype)

        z_ref[...] = project(first)
        sgu_groups(range(n_groups // 2))
        z_ref[...] = z_ref[...] * project(second)
        sgu_groups(range(n_groups // 2, n_groups))

    @pl.when((j == STEP_CX) & (i % blocks_per_seq == 0))
    def _():
        carry_ref[...] = jnp.zeros_like(carry_ref)

    @pl.when(j == STEP_QK)
    def _():
        qk_ref[...] = project().astype(qk_ref.dtype)

    @pl.when(j > STEP_CX)
    def _():
        z = z_ref[...]
        prev1 = carry_ref[7:8, :]
        prev2 = carry_ref[6:7, :]
        row = lax.broadcasted_iota(jnp.int32, z.shape, 0)
        z1 = jnp.where(row == 0, prev1, pltpu.roll(z, 1, 0))
        z2 = jnp.where(row == 0, prev2, jnp.where(row == 1, prev1, pltpu.roll(z, 2, 0)))
        conv = cw_ref[0:1, :] * z2 + cw_ref[1:2, :] * z1 + cw_ref[2:3, :] * z
        yc_ref[...] = (cb_ref[...] * conv).astype(yc_ref.dtype)
        carry_ref[...] = z[bm - 8:bm, :]


def mixer_in(n, w, layer, sgu_norm, sgu_w_s, sgu_b, conv_w, *, seq, bm_target=512):
    m, d = n.shape
    width = d // 2
    assert w.shape[1] == d and w.shape[2] >= 2 * MIX_PAIRS * width
    bm = _pick(seq, bm_target)
    assert bm % SGU_CHUNK == 0 and bm >= 8
    n_groups = width // HEAD_DIM
    const2 = lambda i, j: (0, 0)
    const3 = lambda i, j: (0, 0, 0)
    row_blk = lambda i, j: (i, 0)
    return pl.pallas_call(
        functools.partial(_mix_in_kernel, blocks_per_seq=seq // bm),
        grid=(m // bm, MIX_PAIRS),
        in_specs=[pl.BlockSpec((bm, d), row_blk),
                  _layer_spec((d, 2 * width), layer, lambda i, j: (0, (j + MIX_FIRST_PAIR) % MIX_PAIRS)),
                  pl.BlockSpec((1, width), const2),
                  pl.BlockSpec((n_groups, SGU_CHUNK, SGU_CHUNK), const3),
                  pl.BlockSpec((n_groups, SGU_CHUNK, 1), const3),
                  pl.BlockSpec((CONV_TAPS, width), const2)],
        out_specs=[pl.BlockSpec((bm, 2 * width), row_blk),
                   pl.BlockSpec((bm, width), row_blk),
                   pl.BlockSpec((bm, width), row_blk),
                   pl.BlockSpec((bm, width), row_blk)],
        out_shape=[jax.ShapeDtypeStruct((m, 2 * width), BF16),
                   jax.ShapeDtypeStruct((m, width), BF16),
                   jax.ShapeDtypeStruct((m, width), BF16),
                   jax.ShapeDtypeStruct((m, width), BF16)],
        scratch_shapes=[pltpu.VMEM((bm, width), F32),
                        pltpu.VMEM((bm, width), BF16),
                        pltpu.VMEM((bm, width), F32),
                        pltpu.VMEM((bm, width), F32),
                        pltpu.VMEM((8, width), F32)],
        compiler_params=_params("arbitrary", "arbitrary"),
        name="mixer_in",
    )(n, w, sgu_norm.reshape(1, width), sgu_w_s, sgu_b.reshape(n_groups, SGU_CHUNK, 1), conv_w)


def _bf16_split3(x):
    p1 = x.astype(BF16).astype(F32)
    r1 = x - p1
    p2 = r1.astype(BF16).astype(F32)
    p3 = (r1 - p2).astype(BF16).astype(F32)
    return p1, p2, p3


BF16_SUBLANES = 16


def _diff_attn_kernel(scal_ref, q_ref, k_ref, v_ref, lam_ref, sub_ref, *rest, n_heads, tq, n_cast):
    cast_in, o_ref, cast_out = rest[:n_cast], rest[n_cast], rest[n_cast + 1:2 * n_cast + 1]
    kt_ref, vx_ref = rest[2 * n_cast + 1:]
    for src, dst in zip(cast_in, cast_out):
        dst[...] = src[...].astype(dst.dtype)

    heads_here = kt_ref.shape[0]
    lam_init = scal_ref[n_heads]
    half = HEAD_DIM // 2
    seq = q_ref.shape[1]

    pos = lax.broadcasted_iota(jnp.int32, (HEAD_DIM, seq), 1)
    frow = lax.broadcasted_iota(jnp.int32, (HEAD_DIM, seq), 0)
    pos_hi = (pos - pos % HEAD_DIM).astype(F32)
    pos_lo = (pos % HEAD_DIM).astype(F32)
    pos_rows = jnp.where(frow < 3, pos_hi, jnp.where(frow < 6, pos_lo, 0.0)).astype(BF16)
    for hh in range(heads_here):
        cols = slice(hh * HEAD_DIM, (hh + 1) * HEAD_DIM)
        kt_ref[hh, :HEAD_DIM, :] = k_ref[0, :, cols].T
        kt_ref[hh, HEAD_DIM:, :] = pos_rows
        vx_ref[hh, :, :HEAD_DIM] = v_ref[0, :, cols]
        vx_ref[hh, :, HEAD_DIM:] = jnp.ones((seq, HEAD_DIM), BF16)

    lp = lam_ref[...]
    lam = (jnp.exp(jnp.sum(lp[0:1] * lp[1:2], axis=-1, keepdims=True))
           - jnp.exp(jnp.sum(lp[2:3] * lp[3:4], axis=-1, keepdims=True)) + lam_init)
    sub = sub_ref[...]

    lane = lax.broadcasted_iota(jnp.int32, (tq, HEAD_DIM), 1)
    flane = lax.broadcasted_iota(jnp.int32, (2 * tq, HEAD_DIM), 1)
    r_idx = lax.broadcasted_iota(jnp.int32, (2 * tq, tq), 0)
    c_idx = lax.broadcasted_iota(jnp.int32, (2 * tq, tq), 1)
    causal = c_idx <= jnp.where(r_idx >= tq, r_idx - tq, r_idx)

    for hh in range(heads_here):
        cols = slice(hh * HEAD_DIM, (hh + 1) * HEAD_DIM)
        slope = scal_ref[pl.program_id(1) * heads_here + hh]
        s1, s2, s3 = _bf16_split3(jnp.full((2 * tq, HEAD_DIM), slope, F32))
        piece = jnp.where(flane % 3 == 0, s1, jnp.where(flane % 3 == 1, s2, s3))
        q_feat = jnp.where(flane < 6, piece, 0.0).astype(BF16)

        for i in reversed(range(seq // tq)):
            start, stop = i * tq, (i + 1) * tq
            q = q_ref[0, start:stop, cols].astype(F32) * (1.0 / math.sqrt(half))
            qq = jnp.concatenate([jnp.where(lane < half, q, 0.0),
                                  jnp.where(lane >= half, q, 0.0)], axis=0).astype(BF16)
            t = _dot(jnp.concatenate([qq, q_feat], axis=1), kt_ref[hh, :, :stop])
            t_diag = jnp.where(causal, t[:, start:], NEG_BIG)
            m = jnp.max(t_diag, axis=-1, keepdims=True)
            if i:
                t_past = t[:, :start]
                m = jnp.maximum(m, jnp.max(t_past, axis=-1, keepdims=True))
                p = jnp.concatenate([jnp.exp(t_past - m), jnp.exp(t_diag - m)], axis=1)
            else:
                p = jnp.exp(t_diag - m)
            a = _dot(p.astype(BF16), vx_ref[hh, :stop, :])
            a = a[:, :HEAD_DIM] / a[:, HEAD_DIM:]
            o = a[:tq] - lam * a[tq:]
            o_ref[0, start:stop, cols] = (_rms_rows(o, sub) * (1.0 - lam_init)).astype(o_ref.dtype)


def diff_attention(qk, v, lam_params, subln, lam_init, cast_stacks=(), cast_layer=0, *,
                   batch, seq, tq_target=256):
    width = v.shape[1]
    n_heads = width // HEAD_DIM
    tq = _pick(seq, tq_target)
    slopes = 2.0 ** (-8.0 * jnp.arange(1, n_heads + 1, dtype=F32) / n_heads)
    scal = jnp.concatenate([slopes, jnp.full((1,), lam_init, F32)])
    qk3 = qk.reshape(batch, seq, 2 * width)
    v3 = v.reshape(batch, seq, width)
    hps = _pick(n_heads, 2)
    groups = n_heads // hps
    head_blk = lambda off: pl.BlockSpec((1, seq, hps * HEAD_DIM), lambda b, h: (b, 0, off + h))

    steps = batch * groups
    cast_in_specs, cast_out_specs, cast_out_shapes = [], [], []
    for w in cast_stacks:
        _, rows, cols = w.shape
        n_blocks = steps
        while rows % n_blocks or (rows // n_blocks) % BF16_SUBLANES:
            n_blocks //= 2
        assert n_blocks >= 1 and steps % n_blocks == 0
        hold = steps // n_blocks
        blk = (None, rows // n_blocks, cols)
        cast_in_specs.append(pl.BlockSpec(blk, lambda b, h, hold=hold: (cast_layer, (b * groups + h) // hold, 0)))
        cast_out_specs.append(pl.BlockSpec(blk, lambda b, h, hold=hold: (0, (b * groups + h) // hold, 0)))
        cast_out_shapes.append(jax.ShapeDtypeStruct((1, rows, cols), BF16))

    out = pl.pallas_call(
        functools.partial(_diff_attn_kernel, n_heads=n_heads, tq=tq, n_cast=len(cast_stacks)),
        grid=(batch, groups),
        in_specs=[pl.BlockSpec(memory_space=pltpu.SMEM),
                  head_blk(0), head_blk(groups), head_blk(0),
                  pl.BlockSpec(lam_params.shape, lambda b, h: (0, 0)),
                  pl.BlockSpec((1, HEAD_DIM), lambda b, h: (0, 0))] + cast_in_specs,
        out_specs=[head_blk(0)] + cast_out_specs,
        out_shape=[jax.ShapeDtypeStruct((batch, seq, width), BF16)] + cast_out_shapes,
        scratch_shapes=[pltpu.VMEM((hps, 2 * HEAD_DIM, seq), BF16),
                        pltpu.VMEM((hps, seq, 2 * HEAD_DIM), BF16)],
        compiler_params=_params("parallel", "parallel"),
        name="diff_attention",
    )(scal, qk3, qk3, v3, lam_params, subln.reshape(1, HEAD_DIM), *cast_stacks)
    return out[0].reshape(batch * seq, width), list(out[1:])


def _merge_kernel(n_ref, wg0_ref, wg1_ref, wg2_ref, ya_ref, yb_ref, yc_ref,
                  wa_ref, wb_ref, wc_ref, o_ref):
    n = n_ref[...]

    def branch(wg_ref, y_ref, w_ref):
        return jax.nn.sigmoid(_dot(n, wg_ref[...])) * _dot(y_ref[...], w_ref[...])

    merged = branch(wg0_ref, ya_ref, wa_ref) + branch(wg1_ref, yb_ref, wb_ref)
    o_ref[...] = (merged + branch(wg2_ref, yc_ref, wc_ref)).astype(o_ref.dtype)


def gated_merge(n, w_mix_in, layer, y_a, y_b, y_c, w_a, w_b, w_c, *, bm_target=1024, bn_target=512):
    m, d = n.shape
    width = y_a.shape[1]
    bm, bn = _pick(m, bm_target), _pick(d, bn_target)
    nb = d // bn
    gate0 = (w_mix_in.shape[2] - N_BRANCHES * d) // bn
    row_blk = lambda i, j: (i, 0)
    col_blk = lambda i, j: (0, j)
    return pl.pallas_call(
        _merge_kernel,
        grid=(m // bm, nb),
        in_specs=[pl.BlockSpec((bm, d), row_blk),
                  _layer_spec((d, bn), layer, lambda i, j: (0, gate0 + j)),
                  _layer_spec((d, bn), layer, lambda i, j: (0, gate0 + nb + j)),
                  _layer_spec((d, bn), layer, lambda i, j: (0, gate0 + 2 * nb + j)),
                  pl.BlockSpec((bm, width), row_blk),
                  pl.BlockSpec((bm, width), row_blk),
                  pl.BlockSpec((bm, width), row_blk),
                  _layer_spec((width, bn), layer, col_blk),
                  _layer_spec((width, bn), layer, col_blk),
                  _layer_spec((width, bn), layer, col_blk)],
        out_specs=pl.BlockSpec((bm, bn), lambda i, j: (i, j)),
        out_shape=jax.ShapeDtypeStruct((m, d), BF16),
        compiler_params=_params("parallel", "arbitrary"),
        name="gated_merge",
    )(n, w_mix_in, w_mix_in, w_mix_in, y_a, y_b, y_c, w_a, w_b, w_c)


def _mem_attn_kernel(x_ref, n_ref, wq_ref, kv_ref, wo_ref, g_next_ref, o_ref, n_next_ref):
    mem_width = wq_ref.shape[1]
    q = _dot(n_ref[...], wq_ref[...]).astype(BF16)
    kv = kv_ref[0]
    scale = 1.0 / math.sqrt(HEAD_DIM)
    heads = []
    for hd in range(mem_width // HEAD_DIM):
        cols = slice(hd * HEAD_DIM, (hd + 1) * HEAD_DIM)
        k_h = kv[:, hd * HEAD_DIM:(hd + 1) * HEAD_DIM]
        v_h = kv[:, mem_width + hd * HEAD_DIM:mem_width + (hd + 1) * HEAD_DIM]
        s = _dot_t(q[:, cols], k_h) * scale
        p = jnp.exp(s - jnp.max(s, axis=-1, keepdims=True))
        denom = jnp.sum(p, axis=-1, keepdims=True)
        heads.append(_dot(p.astype(BF16), v_h) / denom)
    o = jnp.concatenate(heads, axis=1).astype(BF16)
    h = x_ref[...] + _dot(o, wo_ref[...])
    o_ref[...] = h
    n_next_ref[...] = _rms_rows(h, g_next_ref[...]).astype(n_next_ref.dtype)


def memory_attention(x, n, w_q, kv, w_o, layer, gain_next, *, seq, bm_target=512):
    m, d = x.shape
    mem_width = w_q.shape[2]
    bm = _pick(seq, bm_target)
    per_seq = seq // bm
    return pl.pallas_call(
        _mem_attn_kernel,
        grid=(m // bm,),
        in_specs=[pl.BlockSpec((bm, d), lambda i: (i, 0)),
                  pl.BlockSpec((bm, d), lambda i: (i, 0)),
                  _layer_spec((d, mem_width), layer, lambda i: (0, 0)),
                  pl.BlockSpec((1,) + kv.shape[1:], lambda i: (i // per_seq, 0, 0)),
                  _layer_spec((mem_width, d), layer, lambda i: (0, 0)),
                  pl.BlockSpec((1, d), lambda i: (0, 0))],
        out_specs=[pl.BlockSpec((bm, d), lambda i: (i, 0)),
                   pl.BlockSpec((bm, d), lambda i: (i, 0))],
        out_shape=[jax.ShapeDtypeStruct((m, d), F32),
                   jax.ShapeDtypeStruct((m, d), BF16)],
        compiler_params=_params("parallel"),
        name="memory_attention",
    )(x, n, w_q, kv, w_o, gain_next.reshape(1, d))


def kernel(x, mem, ffn1_norm, ffn1_w_in, ffn1_w_out, mix_norm, mix_w_in, diff_lambda, diff_subln, diff_w_out, sgu_norm, sgu_w_s, sgu_b, sgu_w_out, conv_w, conv_w_out, mix_w_o, xattn_norm, mem_norm, xattn_w_q, xattn_w_kv, xattn_w_o, ffn2_norm, ffn2_w_in, ffn2_w_out, final_norm):
    batch, seq, d = x.shape
    n_mem = mem.shape[1]
    depth = ffn1_norm.shape[0]
    h = x.reshape(batch * seq, d)
    mem2 = mem.reshape(batch * n_mem, d)

    big = dict(ffn1_w_in=ffn1_w_in, ffn1_w_out=ffn1_w_out, ffn2_w_in=ffn2_w_in, ffn2_w_out=ffn2_w_out,
               mix_w_in=mix_w_in, diff_w_out=diff_w_out, sgu_w_out=sgu_w_out, conv_w_out=conv_w_out,
               mix_w_o=mix_w_o)
    w = {name: stack[0:1].astype(BF16) for name, stack in big.items()}
    xattn_w_q, xattn_w_kv, xattn_w_o = (s.astype(BF16) for s in (xattn_w_q, xattn_w_kv, xattn_w_o))

    hidden = norm_swiglu_in(h, ffn1_norm[0], w["ffn1_w_in"], 0)
    for l in range(depth):
        lam_init = 0.8 - 0.6 * math.exp(-0.3 * l)
        last = l + 1 == depth

        h, n_mix = matmul_residual_norm(hidden, w["ffn1_w_out"], 0, h, 0.5, mix_norm[l])

        qk, v, y_b, y_c = mixer_in(n_mix, w["mix_w_in"], 0, sgu_norm[l], sgu_w_s[l], sgu_b[l], conv_w[l],
                                   seq=seq)
        y_a, casted = diff_attention(qk, v, diff_lambda[l], diff_subln[l], lam_init,
                                     () if last else tuple(big.values()), l + 1, batch=batch, seq=seq)
        merged = gated_merge(n_mix, w["mix_w_in"], 0, y_a, y_b, y_c,
                             w["diff_w_out"], w["sgu_w_out"], w["conv_w_out"])
        h, n_xattn = matmul_residual_norm(merged, w["mix_w_o"], 0, h, 1.0, xattn_norm[l], bm_target=512)

        kv = norm_matmul(mem2, mem_norm[l], xattn_w_kv, l).reshape(batch, n_mem, -1)
        h, n_ffn2 = memory_attention(h, n_xattn, xattn_w_q, kv, xattn_w_o, l, ffn2_norm[l], seq=seq)

        hidden = swiglu_in(n_ffn2, w["ffn2_w_in"], 0)
        if last:
            out = matmul_residual_norm(hidden, w["ffn2_w_out"], 0, h, 0.5, final_norm, final=True)
        else:
            h, n_ffn1 = matmul_residual_norm(hidden, w["ffn2_w_out"], 0, h, 0.5, ffn1_norm[l + 1])
            w = dict(zip(big, casted))
            hidden = swiglu_in(n_ffn1, w["ffn1_w_in"], 0)

    return out.reshape(batch, seq, d)
```

```python
import functools
import math

import jax
import jax.numpy as jnp
from jax import lax
from jax.experimental import pallas as pl
from jax.experimental.pallas import tpu as pltpu

HEAD_DIM = 128
SGU_CHUNK = 128
N_BRANCHES = 3
CONV_TAPS = 3
RMS_EPS = 1e-6
NEG_BIG = -1e30
V7X_VMEM_LIMIT_BYTES = 60 * 1024 * 1024
NORM_ROWS = 256

BF16 = jnp.bfloat16
F32 = jnp.float32


def _params(*sem):
    return pltpu.CompilerParams(dimension_semantics=sem, vmem_limit_bytes=V7X_VMEM_LIMIT_BYTES)


def _pick(total, target):
    b = min(total, target)
    while total % b:
        b //= 2
    return b


def _rms_rows(x, g):
    ms = jnp.mean(x * x, axis=-1, keepdims=True)
    return x * lax.rsqrt(ms + RMS_EPS) * g


def _fill_norm(x_ref, g_ref, n_ref):
    bm = x_ref.shape[0]
    rows = min(NORM_ROWS, bm)
    g = g_ref[...]

    def body(c, carry):
        r0 = pl.multiple_of(c * rows, rows)
        n_ref[pl.ds(r0, rows), :] = _rms_rows(x_ref[pl.ds(r0, rows), :], g).astype(n_ref.dtype)
        return carry

    lax.fori_loop(0, bm // rows, body, 0)


def _dot(a, b):
    return jnp.dot(a, b, preferred_element_type=F32)


def _dot_t(a, b):
    return lax.dot_general(a, b, (((1,), (1,)), ((), ())), preferred_element_type=F32)


def _gelu(x):
    return 0.5 * x * (1.0 + lax.erf(x * (1.0 / math.sqrt(2.0))))


def _norm_mm_kernel(x_ref, g_ref, w_ref, o_ref, n_ref):
    @pl.when(pl.program_id(1) == 0)
    def _():
        _fill_norm(x_ref, g_ref, n_ref)

    o_ref[...] = _dot(n_ref[...], w_ref[...]).astype(o_ref.dtype)


def _layer_spec(block, layer, index_map, *, single_buffer=False):
    return pl.BlockSpec((None,) + block, lambda *g: (layer,) + index_map(*g),
                        pipeline_mode=pl.Buffered(1) if single_buffer else None)


def norm_matmul(x, gain, w, layer, *, bm_target=1024, bn_target=1024):
    m, d = x.shape
    n = w.shape[2]
    bm, bn = _pick(m, bm_target), _pick(n, bn_target)
    return pl.pallas_call(
        _norm_mm_kernel,
        grid=(m // bm, n // bn),
        in_specs=[pl.BlockSpec((bm, d), lambda i, j: (i, 0)),
                  pl.BlockSpec((1, d), lambda i, j: (0, 0)),
                  _layer_spec((d, bn), layer, lambda i, j: (0, j))],
        out_specs=pl.BlockSpec((bm, bn), lambda i, j: (i, j)),
        out_shape=jax.ShapeDtypeStruct((m, n), BF16),
        scratch_shapes=[pltpu.VMEM((bm, d), BF16)],
        compiler_params=_params("parallel", "arbitrary"),
        name="norm_matmul",
    )(x, gain.reshape(1, d), w)


def _swiglu_kernel(n_ref, wa_ref, wb_ref, o_ref):
    n = n_ref[...]
    a = _dot(n, wa_ref[...])
    b = _dot(n, wb_ref[...])
    o_ref[...] = (a * jax.nn.sigmoid(a) * b).astype(o_ref.dtype)


def _norm_swiglu_kernel(x_ref, g_ref, wa_ref, wb_ref, o_ref, n_ref):
    @pl.when(pl.program_id(1) == 0)
    def _():
        _fill_norm(x_ref, g_ref, n_ref)

    _swiglu_kernel(n_ref, wa_ref, wb_ref, o_ref)


def norm_swiglu_in(x, gain, w_in, layer, *, bm_target=1024, bn_target=512):
    m, d = x.shape
    f = w_in.shape[2] // 2
    bm, bn = _pick(m, bm_target), _pick(f, bn_target)
    nb = f // bn
    return pl.pallas_call(
        _norm_swiglu_kernel,
        grid=(m // bm, nb),
        in_specs=[pl.BlockSpec((bm, d), lambda i, j: (i, 0)),
                  pl.BlockSpec((1, d), lambda i, j: (0, 0)),
                  _layer_spec((d, bn), layer, lambda i, j: (0, j)),
                  _layer_spec((d, bn), layer, lambda i, j: (0, j + nb))],
        out_specs=pl.BlockSpec((bm, bn), lambda i, j: (i, j)),
        out_shape=jax.ShapeDtypeStruct((m, f), BF16),
        scratch_shapes=[pltpu.VMEM((bm, d), BF16)],
        compiler_params=_params("parallel", "arbitrary"),
        name="norm_swiglu_in",
    )(x, gain.reshape(1, d), w_in, w_in)


def swiglu_in(n, w_in, layer, *, bm_target=1024, bn_target=512):
    m, d = n.shape
    f = w_in.shape[2] // 2
    bm, bn = _pick(m, bm_target), _pick(f, bn_target)
    nb = f // bn
    return pl.pallas_call(
        _swiglu_kernel,
        grid=(m // bm, nb),
        in_specs=[pl.BlockSpec((bm, d), lambda i, j: (i, 0)),
                  _layer_spec((d, bn), layer, lambda i, j: (0, j)),
                  _layer_spec((d, bn), layer, lambda i, j: (0, j + nb))],
        out_specs=pl.BlockSpec((bm, bn), lambda i, j: (i, j)),
        out_shape=jax.ShapeDtypeStruct((m, f), BF16),
        compiler_params=_params("parallel", "parallel"),
        name="swiglu_in",
    )(n, w_in, w_in)


def _mm_res_norm_kernel(x_ref, w_ref, r_ref, g_ref, *out_refs, scale, final):
    h = r_ref[...] + scale * _dot(x_ref[...], w_ref[...])
    normed = _rms_rows(h, g_ref[...])
    if final:
        out_refs[0][...] = normed
    else:
        out_refs[0][...] = h
        out_refs[1][...] = normed.astype(out_refs[1].dtype)


def matmul_residual_norm(x, w, layer, res, scale, gain, *, final=False, bm_target=256):
    m, k = x.shape
    n = w.shape[2]
    bm = _pick(m, bm_target)
    row_blk = lambda i: (i, 0)
    out_specs = [pl.BlockSpec((bm, n), row_blk)]
    out_shape = [jax.ShapeDtypeStruct((m, n), F32)]
    if not final:
        out_specs.append(pl.BlockSpec((bm, n), row_blk))
        out_shape.append(jax.ShapeDtypeStruct((m, n), BF16))
    out = pl.pallas_call(
        functools.partial(_mm_res_norm_kernel, scale=scale, final=final),
        grid=(m // bm,),
        in_specs=[pl.BlockSpec((bm, k), row_blk),
                  _layer_spec((k, n), layer, lambda i: (0, 0), single_buffer=True),
                  pl.BlockSpec((bm, n), row_blk),
                  pl.BlockSpec((1, n), lambda i: (0, 0))],
        out_specs=out_specs,
        out_shape=out_shape,
        compiler_params=_params("parallel"),
        name="matmul_residual_norm",
    )(x, w, res, gain.reshape(1, n))
    return out[0] if final else out


MIX_PAIRS = 4
MIX_FIRST_PAIR = 1
(STEP_VU, STEP_GB, STEP_CX, STEP_QK) = range(MIX_PAIRS)


def _mix_in_kernel(n_ref, w_ref, sgun_ref, ws_ref, sb_ref, cw_ref,
                   qk_ref, v_ref, yb_ref, yc_ref,
                   u_ref, gv_ref, cb_ref, z_ref, carry_ref, *, blocks_per_seq):
    i, j = pl.program_id(0), pl.program_id(1)
    bm, width = u_ref.shape
    n_groups = width // HEAD_DIM
    n_chunks = bm // SGU_CHUNK
    first, second = slice(0, width), slice(width, 2 * width)

    def project(cols=slice(None)):
        return _dot(n_ref[...], w_ref[:, cols])

    @pl.when(j == STEP_VU)
    def _():
        v_ref[...] = project(first).astype(v_ref.dtype)
        u_ref[...] = _gelu(project(second))

    @pl.when(j == STEP_GB)
    def _():
        gv_ref[...] = _rms_rows(_gelu(project(first)), sgun_ref[...]).astype(gv_ref.dtype)
        cb_ref[...] = project(second)

    @pl.when(j == STEP_CX)
    def _():
        t_idx = lax.broadcasted_iota(jnp.int32, (SGU_CHUNK, SGU_CHUNK), 0)
        s_idx = lax.broadcasted_iota(jnp.int32, (SGU_CHUNK, SGU_CHUNK), 1)

        def sgu_groups(groups):
            for g in groups:
                cols = slice(g * HEAD_DIM, (g + 1) * HEAD_DIM)
                w_tril = jnp.where(s_idx <= t_idx, ws_ref[g], 0.0).astype(BF16)
                xg = jnp.concatenate(
                    [gv_ref[c * SGU_CHUNK:(c + 1) * SGU_CHUNK, cols] for c in range(n_chunks)], axis=1)
                mixed = _dot(w_tril, xg) + sb_ref[g]
                for c in range(n_chunks):
                    rows = slice(c * SGU_CHUNK, (c + 1) * SGU_CHUNK)
                    yb_ref[rows, cols] = (
                        u_ref[rows, cols] * mixed[:, c * HEAD_DIM:(c + 1) * HEAD_DIM]).astype(yb_ref.dtype)

        z_ref[...] = project(first)
        sgu_groups(range(n_groups // 2))
        z_ref[...] = z_ref[...] * project(second)
        sgu_groups(range(n_groups // 2, n_groups))

    @pl.when((j == STEP_CX) & (i % blocks_per_seq == 0))
    def _():
        carry_ref[...] = jnp.zeros_like(carry_ref)

    @pl.when(j == STEP_QK)
    def _():
        qk_ref[...] = project().astype(qk_ref.dtype)

    @pl.when(j > STEP_CX)
    def _():
        z = z_ref[...]
        prev1 = carry_ref[7:8, :]
        prev2 = carry_ref[6:7, :]
        row = lax.broadcasted_iota(jnp.int32, z.shape, 0)
        z1 = jnp.where(row == 0, prev1, pltpu.roll(z, 1, 0))
        z2 = jnp.where(row == 0, prev2, jnp.where(row == 1, prev1, pltpu.roll(z, 2, 0)))
        conv = cw_ref[0:1, :] * z2 + cw_ref[1:2, :] * z1 + cw_ref[2:3, :] * z
        yc_ref[...] = (cb_ref[...] * conv).astype(yc_ref.dtype)
        carry_ref[...] = z[bm - 8:bm, :]


def mixer_in(n, w, layer, sgu_norm, sgu_w_s, sgu_b, conv_w, *, seq, bm_target=512):
    m, d = n.shape
    width = d // 2
    assert w.shape[1] == d and w.shape[2] >= 2 * MIX_PAIRS * width
    bm = _pick(seq, bm_target)
    assert bm % SGU_CHUNK == 0 and bm >= 8
    n_groups = width // HEAD_DIM
    const2 = lambda i, j: (0, 0)
    const3 = lambda i, j: (0, 0, 0)
    row_blk = lambda i, j: (i, 0)
    return pl.pallas_call(
        functools.partial(_mix_in_kernel, blocks_per_seq=seq // bm),
        grid=(m // bm, MIX_PAIRS),
        in_specs=[pl.BlockSpec((bm, d), row_blk),
                  _layer_spec((d, 2 * width), layer, lambda i, j: (0, (j + MIX_FIRST_PAIR) % MIX_PAIRS)),
                  pl.BlockSpec((1, width), const2),
                  pl.BlockSpec((n_groups, SGU_CHUNK, SGU_CHUNK), const3),
                  pl.BlockSpec((n_groups, SGU_CHUNK, 1), const3),
                  pl.BlockSpec((CONV_TAPS, width), const2)],
        out_specs=[pl.BlockSpec((bm, 2 * width), row_blk),
                   pl.BlockSpec((bm, width), row_blk),
                   pl.BlockSpec((bm, width), row_blk),
                   pl.BlockSpec((bm, width), row_blk)],
        out_shape=[jax.ShapeDtypeStruct((m, 2 * width), BF16),
                   jax.ShapeDtypeStruct((m, width), BF16),
                   jax.ShapeDtypeStruct((m, width), BF16),
                   jax.ShapeDtypeStruct((m, width), BF16)],
        scratch_shapes=[pltpu.VMEM((bm, width), F32),
                        pltpu.VMEM((bm, width), BF16),
                        pltpu.VMEM((bm, width), F32),
                        pltpu.VMEM((bm, width), F32),
                        pltpu.VMEM((8, width), F32)],
        compiler_params=_params("arbitrary", "arbitrary"),
        name="mixer_in",
    )(n, w, sgu_norm.reshape(1, width), sgu_w_s, sgu_b.reshape(n_groups, SGU_CHUNK, 1), conv_w)


def _bf16_split3(x):
    p1 = x.astype(BF16).astype(F32)
    r1 = x - p1
    p2 = r1.astype(BF16).astype(F32)
    p3 = (r1 - p2).astype(BF16).astype(F32)
    return p1, p2, p3


BF16_SUBLANES = 16


def _diff_attn_kernel(scal_ref, q_ref, k_ref, v_ref, lam_ref, sub_ref, *rest, n_heads, tq, n_cast):
    cast_in, o_ref, cast_out = rest[:n_cast], rest[n_cast], rest[n_cast + 1:2 * n_cast + 1]
    kt_ref, vx_ref = rest[2 * n_cast + 1:]
    heads_here = kt_ref.shape[0]
    lam_init = scal_ref[n_heads]
    half = HEAD_DIM // 2
    seq = q_ref.shape[1]

    pos = lax.broadcasted_iota(jnp.int32, (HEAD_DIM, seq), 1)
    frow = lax.broadcasted_iota(jnp.int32, (HEAD_DIM, seq), 0)
    pos_hi = (pos - pos % HEAD_DIM).astype(F32)
    pos_lo = (pos % HEAD_DIM).astype(F32)
    pos_rows = jnp.where(frow < 3, pos_hi, jnp.where(frow < 6, pos_lo, 0.0)).astype(BF16)

    lp = lam_ref[...]
    lam = (jnp.exp(jnp.sum(lp[0:1] * lp[1:2], axis=-1, keepdims=True))
           - jnp.exp(jnp.sum(lp[2:3] * lp[3:4], axis=-1, keepdims=True)) + lam_init)
    sub = sub_ref[...]

    lane = lax.broadcasted_iota(jnp.int32, (tq, HEAD_DIM), 1)
    flane = lax.broadcasted_iota(jnp.int32, (2 * tq, HEAD_DIM), 1)
    r_idx = lax.broadcasted_iota(jnp.int32, (2 * tq, tq), 0)
    c_idx = lax.broadcasted_iota(jnp.int32, (2 * tq, tq), 1)
    causal = c_idx <= jnp.where(r_idx >= tq, r_idx - tq, r_idx)

    for hh in range(heads_here):
        cols = slice(hh * HEAD_DIM, (hh + 1) * HEAD_DIM)
        kt_ref[hh, :HEAD_DIM, :] = k_ref[0, :, cols].T
        kt_ref[hh, HEAD_DIM:, :] = pos_rows
        vx_ref[hh, :, :HEAD_DIM] = v_ref[0, :, cols]
        vx_ref[hh, :, HEAD_DIM:] = jnp.ones((seq, HEAD_DIM), BF16)
        slope = scal_ref[pl.program_id(1) * heads_here + hh]
        s1, s2, s3 = _bf16_split3(jnp.full((2 * tq, HEAD_DIM), slope, F32))
        piece = jnp.where(flane % 3 == 0, s1, jnp.where(flane % 3 == 1, s2, s3))
        q_feat = jnp.where(flane < 6, piece, 0.0).astype(BF16)

        for i in reversed(range(seq // tq)):
            start, stop = i * tq, (i + 1) * tq
            q = q_ref[0, start:stop, cols].astype(F32) * (1.0 / math.sqrt(half))
            qq = jnp.concatenate([jnp.where(lane < half, q, 0.0),
                                  jnp.where(lane >= half, q, 0.0)], axis=0).astype(BF16)
            t = _dot(jnp.concatenate([qq, q_feat], axis=1), kt_ref[hh, :, :stop])
            t_diag = jnp.where(causal, t[:, start:], NEG_BIG)
            m = jnp.max(t_diag, axis=-1, keepdims=True)
            if i:
                t_past = t[:, :start]
                m = jnp.maximum(m, jnp.max(t_past, axis=-1, keepdims=True))
                p = jnp.concatenate([jnp.exp(t_past - m), jnp.exp(t_diag - m)], axis=1)
            else:
                p = jnp.exp(t_diag - m)
            a = _dot(p.astype(BF16), vx_ref[hh, :stop, :])
            a = a[:, :HEAD_DIM] / a[:, HEAD_DIM:]
            o = a[:tq] - lam * a[tq:]
            o_ref[0, start:stop, cols] = (_rms_rows(o, sub) * (1.0 - lam_init)).astype(o_ref.dtype)

    for src, dst in zip(cast_in, cast_out):
        dst[...] = src[...].astype(dst.dtype)


def diff_attention(qk, v, lam_params, subln, lam_init, cast_stacks=(), cast_layer=0, *,
                   batch, seq, tq_target=256):
    width = v.shape[1]
    n_heads = width // HEAD_DIM
    tq = _pick(seq, tq_target)
    slopes = 2.0 ** (-8.0 * jnp.arange(1, n_heads + 1, dtype=F32) / n_heads)
    scal = jnp.concatenate([slopes, jnp.full((1,), lam_init, F32)])
    qk3 = qk.reshape(batch, seq, 2 * width)
    v3 = v.reshape(batch, seq, width)
    hps = _pick(n_heads, 2)
    groups = n_heads // hps
    head_blk = lambda off: pl.BlockSpec((1, seq, hps * HEAD_DIM), lambda b, h: (b, 0, off + h))

    steps = batch * groups
    cast_in_specs, cast_out_specs, cast_out_shapes = [], [], []
    for w in cast_stacks:
        _, rows, cols = w.shape
        n_blocks = steps
        while rows % n_blocks or (rows // n_blocks) % BF16_SUBLANES:
            n_blocks //= 2
        assert n_blocks >= 1 and steps % n_blocks == 0
        hold = steps // n_blocks
        blk = (None, rows // n_blocks, cols)
        cast_in_specs.append(pl.BlockSpec(blk, lambda b, h, hold=hold: (cast_layer, (b * groups + h) // hold, 0)))
        cast_out_specs.append(pl.BlockSpec(blk, lambda b, h, hold=hold: (0, (b * groups + h) // hold, 0)))
        cast_out_shapes.append(jax.ShapeDtypeStruct((1, rows, cols), BF16))

    out = pl.pallas_call(
        functools.partial(_diff_attn_kernel, n_heads=n_heads, tq=tq, n_cast=len(cast_stacks)),
        grid=(batch, groups),
        in_specs=[pl.BlockSpec(memory_space=pltpu.SMEM),
                  head_blk(0), head_blk(groups), head_blk(0),
                  pl.BlockSpec(lam_params.shape, lambda b, h: (0, 0)),
                  pl.BlockSpec((1, HEAD_DIM), lambda b, h: (0, 0))] + cast_in_specs,
        out_specs=[head_blk(0)] + cast_out_specs,
        out_shape=[jax.ShapeDtypeStruct((batch, seq, width), BF16)] + cast_out_shapes,
        scratch_shapes=[pltpu.VMEM((hps, 2 * HEAD_DIM, seq), BF16),
                        pltpu.VMEM((hps, seq, 2 * HEAD_DIM), BF16)],
        compiler_params=_params("parallel", "parallel"),
        name="diff_attention",
    )(scal, qk3, qk3, v3, lam_params, subln.reshape(1, HEAD_DIM), *cast_stacks)
    return out[0].reshape(batch * seq, width), list(out[1:])


def _merge_kernel(n_ref, wg0_ref, wg1_ref, wg2_ref, ya_ref, yb_ref, yc_ref,
                  wa_ref, wb_ref, wc_ref, o_ref):
    n = n_ref[...]

    def branch(wg_ref, y_ref, w_ref):
        return jax.nn.sigmoid(_dot(n, wg_ref[...])) * _dot(y_ref[...], w_ref[...])

    merged = branch(wg0_ref, ya_ref, wa_ref) + branch(wg1_ref, yb_ref, wb_ref)
    o_ref[...] = (merged + branch(wg2_ref, yc_ref, wc_ref)).astype(o_ref.dtype)


def gated_merge(n, w_mix_in, layer, y_a, y_b, y_c, w_a, w_b, w_c, *, bm_target=1024, bn_target=512):
    m, d = n.shape
    width = y_a.shape[1]
    bm, bn = _pick(m, bm_target), _pick(d, bn_target)
    nb = d // bn
    gate0 = (w_mix_in.shape[2] - N_BRANCHES * d) // bn
    row_blk = lambda i, j: (i, 0)
    col_blk = lambda i, j: (0, j)
    return pl.pallas_call(
        _merge_kernel,
        grid=(m // bm, nb),
        in_specs=[pl.BlockSpec((bm, d), row_blk),
                  _layer_spec((d, bn), layer, lambda i, j: (0, gate0 + j)),
                  _layer_spec((d, bn), layer, lambda i, j: (0, gate0 + nb + j)),
                  _layer_spec((d, bn), layer, lambda i, j: (0, gate0 + 2 * nb + j)),
                  pl.BlockSpec((bm, width), row_blk),
                  pl.BlockSpec((bm, width), row_blk),
                  pl.BlockSpec((bm, width), row_blk),
                  _layer_spec((width, bn), layer, col_blk),
                  _layer_spec((width, bn), layer, col_blk),
                  _layer_spec((width, bn), layer, col_blk)],
        out_specs=pl.BlockSpec((bm, bn), lambda i, j: (i, j)),
        out_shape=jax.ShapeDtypeStruct((m, d), BF16),
        compiler_params=_params("parallel", "arbitrary"),
        name="gated_merge",
    )(n, w_mix_in, w_mix_in, w_mix_in, y_a, y_b, y_c, w_a, w_b, w_c)


def _mem_attn_kernel(x_ref, n_ref, wq_ref, kv_ref, wo_ref, g_next_ref, o_ref, n_next_ref):
    mem_width = wq_ref.shape[1]
    q = _dot(n_ref[...], wq_ref[...]).astype(BF16)
    kv = kv_ref[0]
    scale = 1.0 / math.sqrt(HEAD_DIM)
    heads = []
    for hd in range(mem_width // HEAD_DIM):
        cols = slice(hd * HEAD_DIM, (hd + 1) * HEAD_DIM)
        k_h = kv[:, hd * HEAD_DIM:(hd + 1) * HEAD_DIM]
        v_h = kv[:, mem_width + hd * HEAD_DIM:mem_width + (hd + 1) * HEAD_DIM]
        s = _dot_t(q[:, cols], k_h) * scale
        p = jnp.exp(s - jnp.max(s, axis=-1, keepdims=True))
        denom = jnp.sum(p, axis=-1, keepdims=True)
        heads.append(_dot(p.astype(BF16), v_h) / denom)
    o = jnp.concatenate(heads, axis=1).astype(BF16)
    h = x_ref[...] + _dot(o, wo_ref[...])
    o_ref[...] = h
    n_next_ref[...] = _rms_rows(h, g_next_ref[...]).astype(n_next_ref.dtype)


def memory_attention(x, n, w_q, kv, w_o, layer, gain_next, *, seq, bm_target=512):
    m, d = x.shape
    mem_width = w_q.shape[2]
    bm = _pick(seq, bm_target)
    per_seq = seq // bm
    return pl.pallas_call(
        _mem_attn_kernel,
        grid=(m // bm,),
        in_specs=[pl.BlockSpec((bm, d), lambda i: (i, 0)),
                  pl.BlockSpec((bm, d), lambda i: (i, 0)),
                  _layer_spec((d, mem_width), layer, lambda i: (0, 0)),
                  pl.BlockSpec((1,) + kv.shape[1:], lambda i: (i // per_seq, 0, 0)),
                  _layer_spec((mem_width, d), layer, lambda i: (0, 0)),
                  pl.BlockSpec((1, d), lambda i: (0, 0))],
        out_specs=[pl.BlockSpec((bm, d), lambda i: (i, 0)),
                   pl.BlockSpec((bm, d), lambda i: (i, 0))],
        out_shape=[jax.ShapeDtypeStruct((m, d), F32),
                   jax.ShapeDtypeStruct((m, d), BF16)],
        compiler_params=_params("parallel"),
        name="memory_attention",
    )(x, n, w_q, kv, w_o, gain_next.reshape(1, d))


def kernel(x, mem, ffn1_norm, ffn1_w_in, ffn1_w_out, mix_norm, mix_w_in, diff_lambda, diff_subln, diff_w_out, sgu_norm, sgu_w_s, sgu_b, sgu_w_out, conv_w, conv_w_out, mix_w_o, xattn_norm, mem_norm, xattn_w_q, xattn_w_kv, xattn_w_o, ffn2_norm, ffn2_w_in, ffn2_w_out, final_norm):
    batch, seq, d = x.shape
    n_mem = mem.shape[1]
    depth = ffn1_norm.shape[0]
    h = x.reshape(batch * seq, d)
    mem2 = mem.reshape(batch * n_mem, d)

    big = dict(ffn1_w_in=ffn1_w_in, ffn1_w_out=ffn1_w_out, ffn2_w_in=ffn2_w_in, ffn2_w_out=ffn2_w_out,
               mix_w_in=mix_w_in, diff_w_out=diff_w_out, sgu_w_out=sgu_w_out, conv_w_out=conv_w_out,
               mix_w_o=mix_w_o)
    w = {name: stack[0:1].astype(BF16) for name, stack in big.items()}
    xattn_w_q, xattn_w_kv, xattn_w_o = (s.astype(BF16) for s in (xattn_w_q, xattn_w_kv, xattn_w_o))

    hidden = norm_swiglu_in(h, ffn1_norm[0], w["ffn1_w_in"], 0)
    for l in range(depth):
        lam_init = 0.8 - 0.6 * math.exp(-0.3 * l)
        last = l + 1 == depth

        h, n_mix = matmul_residual_norm(hidden, w["ffn1_w_out"], 0, h, 0.5, mix_norm[l])

        qk, v, y_b, y_c = mixer_in(n_mix, w["mix_w_in"], 0, sgu_norm[l], sgu_w_s[l], sgu_b[l], conv_w[l],
                                   seq=seq)
        y_a, casted = diff_attention(qk, v, diff_lambda[l], diff_subln[l], lam_init,
                                     () if last else tuple(big.values()), l + 1, batch=batch, seq=seq)
        merged = gated_merge(n_mix, w["mix_w_in"], 0, y_a, y_b, y_c,
                             w["diff_w_out"], w["sgu_w_out"], w["conv_w_out"])
        h, n_xattn = matmul_residual_norm(merged, w["mix_w_o"], 0, h, 1.0, xattn_norm[l], bm_target=512)

        kv = norm_matmul(mem2, mem_norm[l], xattn_w_kv, l).reshape(batch, n_mem, -1)
        h, n_ffn2 = memory_attention(h, n_xattn, xattn_w_q, kv, xattn_w_o, l, ffn2_norm[l], seq=seq)

        hidden = swiglu_in(n_ffn2, w["ffn2_w_in"], 0)
        if last:
            out = matmul_residual_norm(hidden, w["ffn2_w_out"], 0, h, 0.5, final_norm, final=True)
        else:
            h, n_ffn1 = matmul_residual_norm(hidden, w["ffn2_w_out"], 0, h, 0.5, ffn1_norm[l + 1])
            w = dict(zip(big, casted))
            hidden = swiglu_in(n_ffn1, w["ffn1_w_in"], 0)

    return out.reshape(batch, seq, d)
```

```python
import functools
import math

import jax
import jax.numpy as jnp
from jax import lax
from jax.experimental import pallas as pl
from jax.experimental.pallas import tpu as pltpu

HEAD_DIM = 128
SGU_CHUNK = 128
N_BRANCHES = 3
CONV_TAPS = 3
RMS_EPS = 1e-6
NEG_BIG = -1e30
V7X_VMEM_LIMIT_BYTES = 60 * 1024 * 1024
NORM_ROWS = 256

BF16 = jnp.bfloat16
F32 = jnp.float32


def _params(*sem):
    return pltpu.CompilerParams(dimension_semantics=sem, vmem_limit_bytes=V7X_VMEM_LIMIT_BYTES)


def _pick(total, target):
    b = min(total, target)
    while total % b:
        b //= 2
    return b


def _rms_rows(x, g):
    ms = jnp.mean(x * x, axis=-1, keepdims=True)
    return x * lax.rsqrt(ms + RMS_EPS) * g


def _fill_norm(x_ref, g_ref, n_ref):
    bm = x_ref.shape[0]
    rows = min(NORM_ROWS, bm)
    g = g_ref[...]

    def body(c, carry):
        r0 = pl.multiple_of(c * rows, rows)
        n_ref[pl.ds(r0, rows), :] = _rms_rows(x_ref[pl.ds(r0, rows), :], g).astype(n_ref.dtype)
        return carry

    lax.fori_loop(0, bm // rows, body, 0)


def _dot(a, b):
    return jnp.dot(a, b, preferred_element_type=F32)


def _dot_t(a, b):
    return lax.dot_general(a, b, (((1,), (1,)), ((), ())), preferred_element_type=F32)


def _gelu(x):
    return 0.5 * x * (1.0 + lax.erf(x * (1.0 / math.sqrt(2.0))))


def _norm_mm_kernel(x_ref, g_ref, w_ref, o_ref, n_ref):
    @pl.when(pl.program_id(1) == 0)
    def _():
        _fill_norm(x_ref, g_ref, n_ref)

    o_ref[...] = _dot(n_ref[...], w_ref[...]).astype(o_ref.dtype)


def _layer_spec(block, layer, index_map, *, single_buffer=False):
    return pl.BlockSpec((None,) + block, lambda *g: (layer,) + index_map(*g),
                        pipeline_mode=pl.Buffered(1) if single_buffer else None)


def norm_matmul(x, gain, w, layer, *, bm_target=1024, bn_target=1024):
    m, d = x.shape
    n = w.shape[2]
    bm, bn = _pick(m, bm_target), _pick(n, bn_target)
    return pl.pallas_call(
        _norm_mm_kernel,
        grid=(m // bm, n // bn),
        in_specs=[pl.BlockSpec((bm, d), lambda i, j: (i, 0)),
                  pl.BlockSpec((1, d), lambda i, j: (0, 0)),
                  _layer_spec((d, bn), layer, lambda i, j: (0, j))],
        out_specs=pl.BlockSpec((bm, bn), lambda i, j: (i, j)),
        out_shape=jax.ShapeDtypeStruct((m, n), BF16),
        scratch_shapes=[pltpu.VMEM((bm, d), BF16)],
        compiler_params=_params("parallel", "arbitrary"),
        name="norm_matmul",
    )(x, gain.reshape(1, d), w)


def _swiglu_kernel(n_ref, wa_ref, wb_ref, o_ref):
    n = n_ref[...]
    a = _dot(n, wa_ref[...])
    b = _dot(n, wb_ref[...])
    o_ref[...] = (a * jax.nn.sigmoid(a) * b).astype(o_ref.dtype)


def _norm_swiglu_kernel(x_ref, g_ref, wa_ref, wb_ref, o_ref, n_ref):
    @pl.when(pl.program_id(1) == 0)
    def _():
        _fill_norm(x_ref, g_ref, n_ref)

    _swiglu_kernel(n_ref, wa_ref, wb_ref, o_ref)


def norm_swiglu_in(x, gain, w_in, layer, *, bm_target=1024, bn_target=512):
    m, d = x.shape
    f = w_in.shape[2] // 2
    bm, bn = _pick(m, bm_target), _pick(f, bn_target)
    nb = f // bn
    return pl.pallas_call(
        _norm_swiglu_kernel,
        grid=(m // bm, nb),
        in_specs=[pl.BlockSpec((bm, d), lambda i, j: (i, 0)),
                  pl.BlockSpec((1, d), lambda i, j: (0, 0)),
                  _layer_spec((d, bn), layer, lambda i, j: (0, j)),
                  _layer_spec((d, bn), layer, lambda i, j: (0, j + nb))],
        out_specs=pl.BlockSpec((bm, bn), lambda i, j: (i, j)),
        out_shape=jax.ShapeDtypeStruct((m, f), BF16),
        scratch_shapes=[pltpu.VMEM((bm, d), BF16)],
        compiler_params=_params("parallel", "arbitrary"),
        name="norm_swiglu_in",
    )(x, gain.reshape(1, d), w_in, w_in)


def swiglu_in(n, w_in, layer, *, bm_target=1024, bn_target=512):
    m, d = n.shape
    f = w_in.shape[2] // 2
    bm, bn = _pick(m, bm_target), _pick(f, bn_target)
    nb = f // bn
    return pl.pallas_call(
        _swiglu_kernel,
        grid=(m // bm, nb),
        in_specs=[pl.BlockSpec((bm, d), lambda i, j: (i, 0)),
                  _layer_spec((d, bn), layer, lambda i, j: (0, j)),
                  _layer_spec((d, bn), layer, lambda i, j: (0, j + nb))],
        out_specs=pl.BlockSpec((bm, bn), lambda i, j: (i, j)),
        out_shape=jax.ShapeDtypeStruct((m, f), BF16),
        compiler_params=_params("parallel", "parallel"),
        name="swiglu_in",
    )(n, w_in, w_in)


def _mm_res_norm_kernel(x_ref, w_ref, r_ref, g_ref, *out_refs, scale, final):
    h = r_ref[...] + scale * _dot(x_ref[...], w_ref[...])
    normed = _rms_rows(h, g_ref[...])
    if final:
        out_refs[0][...] = normed
    else:
        out_refs[0][...] = h
        out_refs[1][...] = normed.astype(out_refs[1].dtype)


def matmul_residual_norm(x, w, layer, res, scale, gain, *, final=False, bm_target=256):
    m, k = x.shape
    n = w.shape[2]
    bm = _pick(m, bm_target)
    row_blk = lambda i: (i, 0)
    out_specs = [pl.BlockSpec((bm, n), row_blk)]
    out_shape = [jax.ShapeDtypeStruct((m, n), F32)]
    if not final:
        out_specs.append(pl.BlockSpec((bm, n), row_blk))
        out_shape.append(jax.ShapeDtypeStruct((m, n), BF16))
    out = pl.pallas_call(
        functools.partial(_mm_res_norm_kernel, scale=scale, final=final),
        grid=(m // bm,),
        in_specs=[pl.BlockSpec((bm, k), row_blk),
                  _layer_spec((k, n), layer, lambda i: (0, 0), single_buffer=True),
                  pl.BlockSpec((bm, n), row_blk),
                  pl.BlockSpec((1, n), lambda i: (0, 0))],
        out_specs=out_specs,
        out_shape=out_shape,
        compiler_params=_params("parallel"),
        name="matmul_residual_norm",
    )(x, w, res, gain.reshape(1, n))
    return out[0] if final else out


MIX_PAIRS = 4
MIX_FIRST_PAIR = 1
(STEP_VU, STEP_GB, STEP_CX, STEP_QK) = range(MIX_PAIRS)


def _mix_in_kernel(n_ref, w_ref, sgun_ref, ws_ref, sb_ref, cw_ref,
                   qk_ref, v_ref, yb_ref, yc_ref,
                   u_ref, gv_ref, cb_ref, z_ref, carry_ref, *, blocks_per_seq):
    i, j = pl.program_id(0), pl.program_id(1)
    bm, width = u_ref.shape
    n_groups = width // HEAD_DIM
    n_chunks = bm // SGU_CHUNK
    first, second = slice(0, width), slice(width, 2 * width)

    def project(cols=slice(None)):
        return _dot(n_ref[...], w_ref[:, cols])

    @pl.when(j == STEP_VU)
    def _():
        v_ref[...] = project(first).astype(v_ref.dtype)
        u_ref[...] = _gelu(project(second))

    @pl.when(j == STEP_GB)
    def _():
        gv_ref[...] = _rms_rows(_gelu(project(first)), sgun_ref[...]).astype(gv_ref.dtype)
        cb_ref[...] = project(second)

    @pl.when(j == STEP_CX)
    def _():
        t_idx = lax.broadcasted_iota(jnp.int32, (SGU_CHUNK, SGU_CHUNK), 0)
        s_idx = lax.broadcasted_iota(jnp.int32, (SGU_CHUNK, SGU_CHUNK), 1)

        def sgu_groups(groups):
            for g in groups:
                cols = slice(g * HEAD_DIM, (g + 1) * HEAD_DIM)
                w_tril = jnp.where(s_idx <= t_idx, ws_ref[g], 0.0).astype(BF16)
                xg = jnp.concatenate(
                    [gv_ref[c * SGU_CHUNK:(c + 1) * SGU_CHUNK, cols] for c in range(n_chunks)], axis=1)
                mixed = _dot(w_tril, xg) + sb_ref[g]
                for c in range(n_chunks):
                    rows = slice(c * SGU_CHUNK, (c + 1) * SGU_CHUNK)
                    yb_ref[rows, cols] = (
                        u_ref[rows, cols] * mixed[:, c * HEAD_DIM:(c + 1) * HEAD_DIM]).astype(yb_ref.dtype)

        z_ref[...] = project(first)
        sgu_groups(range(n_groups // 2))
        z_ref[...] = z_ref[...] * project(second)
        sgu_groups(range(n_groups // 2, n_groups))

    @pl.when((j == STEP_CX) & (i % blocks_per_seq == 0))
    def _():
        carry_ref[...] = jnp.zeros_like(carry_ref)

    @pl.when(j == STEP_QK)
    def _():
        qk_ref[...] = project().astype(qk_ref.dtype)

    @pl.when(j > STEP_CX)
    def _():
        z = z_ref[...]
        prev1 = carry_ref[7:8, :]
        prev2 = carry_ref[6:7, :]
        row = lax.broadcasted_iota(jnp.int32, z.shape, 0)
        z1 = jnp.where(row == 0, prev1, pltpu.roll(z, 1, 0))
        z2 = jnp.where(row == 0, prev2, jnp.where(row == 1, prev1, pltpu.roll(z, 2, 0)))
        conv = cw_ref[0:1, :] * z2 + cw_ref[1:2, :] * z1 + cw_ref[2:3, :] * z
        yc_ref[...] = (cb_ref[...] * conv).astype(yc_ref.dtype)
        carry_ref[...] = z[bm - 8:bm, :]


def mixer_in(n, w, layer, sgu_norm, sgu_w_s, sgu_b, conv_w, *, seq, bm_target=512):
    m, d = n.shape
    width = d // 2
    assert w.shape[1] == d and w.shape[2] >= 2 * MIX_PAIRS * width
    bm = _pick(seq, bm_target)
    assert bm % SGU_CHUNK == 0 and bm >= 8
    n_groups = width // HEAD_DIM
    const2 = lambda i, j: (0, 0)
    const3 = lambda i, j: (0, 0, 0)
    row_blk = lambda i, j: (i, 0)
    return pl.pallas_call(
        functools.partial(_mix_in_kernel, blocks_per_seq=seq // bm),
        grid=(m // bm, MIX_PAIRS),
        in_specs=[pl.BlockSpec((bm, d), row_blk),
                  _layer_spec((d, 2 * width), layer, lambda i, j: (0, (j + MIX_FIRST_PAIR) % MIX_PAIRS)),
                  pl.BlockSpec((1, width), const2),
                  pl.BlockSpec((n_groups, SGU_CHUNK, SGU_CHUNK), const3),
                  pl.BlockSpec((n_groups, SGU_CHUNK, 1), const3),
                  pl.BlockSpec((CONV_TAPS, width), const2)],
        out_specs=[pl.BlockSpec((bm, 2 * width), row_blk),
                   pl.BlockSpec((bm, width), row_blk),
                   pl.BlockSpec((bm, width), row_blk),
                   pl.BlockSpec((bm, width), row_blk)],
        out_shape=[jax.ShapeDtypeStruct((m, 2 * width), BF16),
                   jax.ShapeDtypeStruct((m, width), BF16),
                   jax.ShapeDtypeStruct((m, width), BF16),
                   jax.ShapeDtypeStruct((m, width), BF16)],
        scratch_shapes=[pltpu.VMEM((bm, width), F32),
                        pltpu.VMEM((bm, width), BF16),
                        pltpu.VMEM((bm, width), F32),
                        pltpu.VMEM((bm, width), F32),
                        pltpu.VMEM((8, width), F32)],
        compiler_params=_params("arbitrary", "arbitrary"),
        name="mixer_in",
    )(n, w, sgu_norm.reshape(1, width), sgu_w_s, sgu_b.reshape(n_groups, SGU_CHUNK, 1), conv_w)


def _bf16_split3(x):
    p1 = x.astype(BF16).astype(F32)
    r1 = x - p1
    p2 = r1.astype(BF16).astype(F32)
    p3 = (r1 - p2).astype(BF16).astype(F32)
    return p1, p2, p3


BF16_SUBLANES = 16


def _diff_attn_kernel(scal_ref, q_ref, k_ref, v_ref, lam_ref, sub_ref, *rest, n_heads, tq, n_cast):
    cast_in, o_ref, cast_out = rest[:n_cast], rest[n_cast], rest[n_cast + 1:2 * n_cast + 1]
    kt_ref, vx_ref = rest[2 * n_cast + 1:]
    heads_here = kt_ref.shape[0]
    lam_init = scal_ref[n_heads]
    half = HEAD_DIM // 2
    seq = q_ref.shape[1]

    pos = lax.broadcasted_iota(jnp.int32, (HEAD_DIM, seq), 1)
    frow = lax.broadcasted_iota(jnp.int32, (HEAD_DIM, seq), 0)
    pos_hi = (pos - pos % HEAD_DIM).astype(F32)
    pos_lo = (pos % HEAD_DIM).astype(F32)
    pos_rows = jnp.where(frow < 3, pos_hi, jnp.where(frow < 6, pos_lo, 0.0)).astype(BF16)

    lp = lam_ref[...]
    lam = (jnp.exp(jnp.sum(lp[0:1] * lp[1:2], axis=-1, keepdims=True))
           - jnp.exp(jnp.sum(lp[2:3] * lp[3:4], axis=-1, keepdims=True)) + lam_init)
    sub = sub_ref[...]

    lane = lax.broadcasted_iota(jnp.int32, (tq, HEAD_DIM), 1)
    flane = lax.broadcasted_iota(jnp.int32, (2 * tq, HEAD_DIM), 1)
    r_idx = lax.broadcasted_iota(jnp.int32, (2 * tq, tq), 0)
    c_idx = lax.broadcasted_iota(jnp.int32, (2 * tq, tq), 1)
    causal = c_idx <= jnp.where(r_idx >= tq, r_idx - tq, r_idx)

    for hh in range(heads_here):
        cols = slice(hh * HEAD_DIM, (hh + 1) * HEAD_DIM)
        kt_ref[hh, :HEAD_DIM, :] = k_ref[0, :, cols].T
        kt_ref[hh, HEAD_DIM:, :] = pos_rows
        vx_ref[hh, :, :HEAD_DIM] = v_ref[0, :, cols]
        vx_ref[hh, :, HEAD_DIM:] = jnp.ones((seq, HEAD_DIM), BF16)
        slope = scal_ref[pl.program_id(1) * heads_here + hh]
        s1, s2, s3 = _bf16_split3(jnp.full((2 * tq, HEAD_DIM), slope, F32))
        piece = jnp.where(flane % 3 == 0, s1, jnp.where(flane % 3 == 1, s2, s3))
        q_feat = jnp.where(flane < 6, piece, 0.0).astype(BF16)

        for i in reversed(range(seq // tq)):
            start, stop = i * tq, (i + 1) * tq
            q = q_ref[0, start:stop, cols].astype(F32) * (1.0 / math.sqrt(half))
            qq = jnp.concatenate([jnp.where(lane < half, q, 0.0),
                                  jnp.where(lane >= half, q, 0.0)], axis=0).astype(BF16)
            t = _dot(jnp.concatenate([qq, q_feat], axis=1), kt_ref[hh, :, :stop])
            t_diag = jnp.where(causal, t[:, start:], NEG_BIG)
            m = jnp.max(t_diag, axis=-1, keepdims=True)
            if i:
                t_past = t[:, :start]
                m = jnp.maximum(m, jnp.max(t_past, axis=-1, keepdims=True))
                p = jnp.concatenate([jnp.exp(t_past - m), jnp.exp(t_diag - m)], axis=1)
            else:
                p = jnp.exp(t_diag - m)
            a = _dot(p.astype(BF16), vx_ref[hh, :stop, :])
            a = a[:, :HEAD_DIM] / a[:, HEAD_DIM:]
            o = a[:tq] - lam * a[tq:]
            o_ref[0, start:stop, cols] = (_rms_rows(o, sub) * (1.0 - lam_init)).astype(o_ref.dtype)

    for src, dst in zip(cast_in, cast_out):
        dst[...] = src[...].astype(dst.dtype)


def diff_attention(qk, v, lam_params, subln, lam_init, cast_stacks=(), cast_layer=0, *,
                   batch, seq, tq_target=256):
    width = v.shape[1]
    n_heads = width // HEAD_DIM
    tq = _pick(seq, tq_target)
    slopes = 2.0 ** (-8.0 * jnp.arange(1, n_heads + 1, dtype=F32) / n_heads)
    scal = jnp.concatenate([slopes, jnp.full((1,), lam_init, F32)])
    qk3 = qk.reshape(batch, seq, 2 * width)
    v3 = v.reshape(batch, seq, width)
    hps = _pick(n_heads, 2)
    groups = n_heads // hps
    head_blk = lambda off: pl.BlockSpec((1, seq, hps * HEAD_DIM), lambda b, h: (b, 0, off + h))

    steps = batch * groups
    cast_in_specs, cast_out_specs, cast_out_shapes = [], [], []
    for w in cast_stacks:
        _, rows, cols = w.shape
        n_blocks = steps
        while rows % n_blocks or (rows // n_blocks) % BF16_SUBLANES:
            n_blocks //= 2
        assert n_blocks >= 1 and steps % n_blocks == 0
        hold = steps // n_blocks
        blk = (None, rows // n_blocks, cols)
        cast_in_specs.append(pl.BlockSpec(blk, lambda b, h, hold=hold: (cast_layer, (b * groups + h) // hold, 0)))
        cast_out_specs.append(pl.BlockSpec(blk, lambda b, h, hold=hold: (0, (b * groups + h) // hold, 0)))
        cast_out_shapes.append(jax.ShapeDtypeStruct((1, rows, cols), BF16))

    out = pl.pallas_call(
        functools.partial(_diff_attn_kernel, n_heads=n_heads, tq=tq, n_cast=len(cast_stacks)),
        grid=(batch, groups),
        in_specs=[pl.BlockSpec(memory_space=pltpu.SMEM),
                  head_blk(0), head_blk(groups), head_blk(0),
                  pl.BlockSpec(lam_params.shape, lambda b, h: (0, 0)),
                  pl.BlockSpec((1, HEAD_DIM), lambda b, h: (0, 0))] + cast_in_specs,
        out_specs=[head_blk(0)] + cast_out_specs,
        out_shape=[jax.ShapeDtypeStruct((batch, seq, width), BF16)] + cast_out_shapes,
        scratch_shapes=[pltpu.VMEM((hps, 2 * HEAD_DIM, seq), BF16),
                        pltpu.VMEM((hps, seq, 2 * HEAD_DIM), BF16)],
        compiler_params=_params("arbitrary", "arbitrary"),
        name="diff_attention",
    )(scal, qk3, qk3, v3, lam_params, subln.reshape(1, HEAD_DIM), *cast_stacks)
    return out[0].reshape(batch * seq, width), list(out[1:])


def _merge_kernel(n_ref, wg0_ref, wg1_ref, wg2_ref, ya_ref, yb_ref, yc_ref,
                  wa_ref, wb_ref, wc_ref, o_ref):
    n = n_ref[...]

    def branch(wg_ref, y_ref, w_ref):
        return jax.nn.sigmoid(_dot(n, wg_ref[...])) * _dot(y_ref[...], w_ref[...])

    merged = branch(wg0_ref, ya_ref, wa_ref) + branch(wg1_ref, yb_ref, wb_ref)
    o_ref[...] = (merged + branch(wg2_ref, yc_ref, wc_ref)).astype(o_ref.dtype)


def gated_merge(n, w_mix_in, layer, y_a, y_b, y_c, w_a, w_b, w_c, *, bm_target=1024, bn_target=512):
    m, d = n.shape
    width = y_a.shape[1]
    bm, bn = _pick(m, bm_target), _pick(d, bn_target)
    nb = d // bn
    gate0 = (w_mix_in.shape[2] - N_BRANCHES * d) // bn
    row_blk = lambda i, j: (i, 0)
    col_blk = lambda i, j: (0, j)
    return pl.pallas_call(
        _merge_kernel,
        grid=(m // bm, nb),
        in_specs=[pl.BlockSpec((bm, d), row_blk),
                  _layer_spec((d, bn), layer, lambda i, j: (0, gate0 + j)),
                  _layer_spec((d, bn), layer, lambda i, j: (0, gate0 + nb + j)),
                  _layer_spec((d, bn), layer, lambda i, j: (0, gate0 + 2 * nb + j)),
                  pl.BlockSpec((bm, width), row_blk),
                  pl.BlockSpec((bm, width), row_blk),
                  pl.BlockSpec((bm, width), row_blk),
                  _layer_spec((width, bn), layer, col_blk),
                  _layer_spec((width, bn), layer, col_blk),
                  _layer_spec((width, bn), layer, col_blk)],
        out_specs=pl.BlockSpec((bm, bn), lambda i, j: (i, j)),
        out_shape=jax.ShapeDtypeStruct((m, d), BF16),
        compiler_params=_params("parallel", "arbitrary"),
        name="gated_merge",
    )(n, w_mix_in, w_mix_in, w_mix_in, y_a, y_b, y_c, w_a, w_b, w_c)


def _mem_attn_kernel(x_ref, n_ref, wq_ref, kv_ref, wo_ref, g_next_ref, o_ref, n_next_ref):
    mem_width = wq_ref.shape[1]
    q = _dot(n_ref[...], wq_ref[...]).astype(BF16)
    kv = kv_ref[0]
    scale = 1.0 / math.sqrt(HEAD_DIM)
    heads = []
    for hd in range(mem_width // HEAD_DIM):
        cols = slice(hd * HEAD_DIM, (hd + 1) * HEAD_DIM)
        k_h = kv[:, hd * HEAD_DIM:(hd + 1) * HEAD_DIM]
        v_h = kv[:, mem_width + hd * HEAD_DIM:mem_width + (hd + 1) * HEAD_DIM]
        s = _dot_t(q[:, cols], k_h) * scale
        p = jnp.exp(s - jnp.max(s, axis=-1, keepdims=True))
        denom = jnp.sum(p, axis=-1, keepdims=True)
        heads.append(_dot(p.astype(BF16), v_h) / denom)
    o = jnp.concatenate(heads, axis=1).astype(BF16)
    h = x_ref[...] + _dot(o, wo_ref[...])
    o_ref[...] = h
    n_next_ref[...] = _rms_rows(h, g_next_ref[...]).astype(n_next_ref.dtype)


def memory_attention(x, n, w_q, kv, w_o, layer, gain_next, *, seq, bm_target=512):
    m, d = x.shape
    mem_width = w_q.shape[2]
    bm = _pick(seq, bm_target)
    per_seq = seq // bm
    return pl.pallas_call(
        _mem_attn_kernel,
        grid=(m // bm,),
        in_specs=[pl.BlockSpec((bm, d), lambda i: (i, 0)),
                  pl.BlockSpec((bm, d), lambda i: (i, 0)),
                  _layer_spec((d, mem_width), layer, lambda i: (0, 0)),
                  pl.BlockSpec((1,) + kv.shape[1:], lambda i: (i // per_seq, 0, 0)),
                  _layer_spec((mem_width, d), layer, lambda i: (0, 0)),
                  pl.BlockSpec((1, d), lambda i: (0, 0))],
        out_specs=[pl.BlockSpec((bm, d), lambda i: (i, 0)),
                   pl.BlockSpec((bm, d), lambda i: (i, 0))],
        out_shape=[jax.ShapeDtypeStruct((m, d), F32),
                   jax.ShapeDtypeStruct((m, d), BF16)],
        compiler_params=_params("parallel"),
        name="memory_attention",
    )(x, n, w_q, kv, w_o, gain_next.reshape(1, d))


def kernel(x, mem, ffn1_norm, ffn1_w_in, ffn1_w_out, mix_norm, mix_w_in, diff_lambda, diff_subln, diff_w_out, sgu_norm, sgu_w_s, sgu_b, sgu_w_out, conv_w, conv_w_out, mix_w_o, xattn_norm, mem_norm, xattn_w_q, xattn_w_kv, xattn_w_o, ffn2_norm, ffn2_w_in, ffn2_w_out, final_norm):
    batch, seq, d = x.shape
    n_mem = mem.shape[1]
    depth = ffn1_norm.shape[0]
    h = x.reshape(batch * seq, d)
    mem2 = mem.reshape(batch * n_mem, d)

    big = dict(ffn1_w_in=ffn1_w_in, ffn1_w_out=ffn1_w_out, ffn2_w_in=ffn2_w_in, ffn2_w_out=ffn2_w_out,
               mix_w_in=mix_w_in, diff_w_out=diff_w_out, sgu_w_out=sgu_w_out, conv_w_out=conv_w_out,
               mix_w_o=mix_w_o)
    w = {name: stack[0:1].astype(BF16) for name, stack in big.items()}
    xattn_w_q, xattn_w_kv, xattn_w_o = (s.astype(BF16) for s in (xattn_w_q, xattn_w_kv, xattn_w_o))

    hidden = norm_swiglu_in(h, ffn1_norm[0], w["ffn1_w_in"], 0)
    for l in range(depth):
        lam_init = 0.8 - 0.6 * math.exp(-0.3 * l)
        last = l + 1 == depth

        h, n_mix = matmul_residual_norm(hidden, w["ffn1_w_out"], 0, h, 0.5, mix_norm[l])

        qk, v, y_b, y_c = mixer_in(n_mix, w["mix_w_in"], 0, sgu_norm[l], sgu_w_s[l], sgu_b[l], conv_w[l],
                                   seq=seq)
        y_a, casted = diff_attention(qk, v, diff_lambda[l], diff_subln[l], lam_init,
                                     () if last else tuple(big.values()), l + 1, batch=batch, seq=seq)
        merged = gated_merge(n_mix, w["mix_w_in"], 0, y_a, y_b, y_c,
                             w["diff_w_out"], w["sgu_w_out"], w["conv_w_out"])
        h, n_xattn = matmul_residual_norm(merged, w["mix_w_o"], 0, h, 1.0, xattn_norm[l], bm_target=512)

        kv = norm_matmul(mem2, mem_norm[l], xattn_w_kv, l).reshape(batch, n_mem, -1)
        h, n_ffn2 = memory_attention(h, n_xattn, xattn_w_q, kv, xattn_w_o, l, ffn2_norm[l], seq=seq)

        hidden = swiglu_in(n_ffn2, w["ffn2_w_in"], 0)
        if last:
            out = matmul_residual_norm(hidden, w["ffn2_w_out"], 0, h, 0.5, final_norm, final=True)
        else:
            h, n_ffn1 = matmul_residual_norm(hidden, w["ffn2_w_out"], 0, h, 0.5, ffn1_norm[l + 1])
            w = dict(zip(big, casted))
            hidden = swiglu_in(n_ffn1, w["ffn1_w_in"], 0)

    return out.reshape(batch, seq, d)
```

```python
import functools
import math

import jax
import jax.numpy as jnp
from jax import lax
from jax.experimental import pallas as pl
from jax.experimental.pallas import tpu as pltpu

HEAD_DIM = 128
SGU_CHUNK = 128
N_BRANCHES = 3
CONV_TAPS = 3
RMS_EPS = 1e-6
NEG_BIG = -1e30
V7X_VMEM_LIMIT_BYTES = 60 * 1024 * 1024
NORM_ROWS = 256

BF16 = jnp.bfloat16
F32 = jnp.float32


def _params(*sem):
    return pltpu.CompilerParams(dimension_semantics=sem, vmem_limit_bytes=V7X_VMEM_LIMIT_BYTES)


def _pick(total, target):
    b = min(total, target)
    while total % b:
        b //= 2
    return b


def _rms_rows(x, g):
    ms = jnp.mean(x * x, axis=-1, keepdims=True)
    return x * lax.rsqrt(ms + RMS_EPS) * g


def _fill_norm(x_ref, g_ref, n_ref):
    bm = x_ref.shape[0]
    rows = min(NORM_ROWS, bm)
    g = g_ref[...]

    def body(c, carry):
        r0 = pl.multiple_of(c * rows, rows)
        n_ref[pl.ds(r0, rows), :] = _rms_rows(x_ref[pl.ds(r0, rows), :], g).astype(n_ref.dtype)
        return carry

    lax.fori_loop(0, bm // rows, body, 0)


def _dot(a, b):
    return jnp.dot(a, b, preferred_element_type=F32)


def _dot_t(a, b):
    return lax.dot_general(a, b, (((1,), (1,)), ((), ())), preferred_element_type=F32)


def _gelu(x):
    return 0.5 * x * (1.0 + lax.erf(x * (1.0 / math.sqrt(2.0))))


def _norm_mm_kernel(x_ref, g_ref, w_ref, o_ref, n_ref):
    @pl.when(pl.program_id(1) == 0)
    def _():
        _fill_norm(x_ref, g_ref, n_ref)

    o_ref[...] = _dot(n_ref[...], w_ref[...]).astype(o_ref.dtype)


def _layer_spec(block, layer, index_map, *, single_buffer=False):
    return pl.BlockSpec((None,) + block, lambda *g: (layer,) + index_map(*g),
                        pipeline_mode=pl.Buffered(1) if single_buffer else None)


def norm_matmul(x, gain, w, layer, *, bm_target=1024, bn_target=1024):
    m, d = x.shape
    n = w.shape[2]
    bm, bn = _pick(m, bm_target), _pick(n, bn_target)
    return pl.pallas_call(
        _norm_mm_kernel,
        grid=(m // bm, n // bn),
        in_specs=[pl.BlockSpec((bm, d), lambda i, j: (i, 0)),
                  pl.BlockSpec((1, d), lambda i, j: (0, 0)),
                  _layer_spec((d, bn), layer, lambda i, j: (0, j))],
        out_specs=pl.BlockSpec((bm, bn), lambda i, j: (i, j)),
        out_shape=jax.ShapeDtypeStruct((m, n), BF16),
        scratch_shapes=[pltpu.VMEM((bm, d), BF16)],
        compiler_params=_params("parallel", "arbitrary"),
        name="norm_matmul",
    )(x, gain.reshape(1, d), w)


def _swiglu_kernel(n_ref, wa_ref, wb_ref, o_ref):
    n = n_ref[...]
    a = _dot(n, wa_ref[...])
    b = _dot(n, wb_ref[...])
    o_ref[...] = (a * jax.nn.sigmoid(a) * b).astype(o_ref.dtype)


def _norm_swiglu_kernel(x_ref, g_ref, wa_ref, wb_ref, o_ref, n_ref):
    @pl.when(pl.program_id(1) == 0)
    def _():
        _fill_norm(x_ref, g_ref, n_ref)

    _swiglu_kernel(n_ref, wa_ref, wb_ref, o_ref)


def norm_swiglu_in(x, gain, w_in, layer, *, bm_target=1024, bn_target=512):
    m, d = x.shape
    f = w_in.shape[2] // 2
    bm, bn = _pick(m, bm_target), _pick(f, bn_target)
    nb = f // bn
    return pl.pallas_call(
        _norm_swiglu_kernel,
        grid=(m // bm, nb),
        in_specs=[pl.BlockSpec((bm, d), lambda i, j: (i, 0)),
                  pl.BlockSpec((1, d), lambda i, j: (0, 0)),
                  _layer_spec((d, bn), layer, lambda i, j: (0, j)),
                  _layer_spec((d, bn), layer, lambda i, j: (0, j + nb))],
        out_specs=pl.BlockSpec((bm, bn), lambda i, j: (i, j)),
        out_shape=jax.ShapeDtypeStruct((m, f), BF16),
        scratch_shapes=[pltpu.VMEM((bm, d), BF16)],
        compiler_params=_params("parallel", "arbitrary"),
        name="norm_swiglu_in",
    )(x, gain.reshape(1, d), w_in, w_in)


def swiglu_in(n, w_in, layer, *, bm_target=1024, bn_target=512):
    m, d = n.shape
    f = w_in.shape[2] // 2
    bm, bn = _pick(m, bm_target), _pick(f, bn_target)
    nb = f // bn
    return pl.pallas_call(
        _swiglu_kernel,
        grid=(m // bm, nb),
        in_specs=[pl.BlockSpec((bm, d), lambda i, j: (i, 0)),
                  _layer_spec((d, bn), layer, lambda i, j: (0, j)),
                  _layer_spec((d, bn), layer, lambda i, j: (0, j + nb))],
        out_specs=pl.BlockSpec((bm, bn), lambda i, j: (i, j)),
        out_shape=jax.ShapeDtypeStruct((m, f), BF16),
        compiler_params=_params("parallel", "parallel"),
        name="swiglu_in",
    )(n, w_in, w_in)


def _mm_res_norm_kernel(x_ref, w_ref, r_ref, g_ref, *out_refs, scale, final):
    h = r_ref[...] + scale * _dot(x_ref[...], w_ref[...])
    normed = _rms_rows(h, g_ref[...])
    if final:
        out_refs[0][...] = normed
    else:
        out_refs[0][...] = h
        out_refs[1][...] = normed.astype(out_refs[1].dtype)


def matmul_residual_norm(x, w, layer, res, scale, gain, *, final=False, bm_target=512):
    m, k = x.shape
    n = w.shape[2]
    bm = _pick(m, bm_target)
    row_blk = lambda i: (i, 0)
    out_specs = [pl.BlockSpec((bm, n), row_blk)]
    out_shape = [jax.ShapeDtypeStruct((m, n), F32)]
    if not final:
        out_specs.append(pl.BlockSpec((bm, n), row_blk))
        out_shape.append(jax.ShapeDtypeStruct((m, n), BF16))
    out = pl.pallas_call(
        functools.partial(_mm_res_norm_kernel, scale=scale, final=final),
        grid=(m // bm,),
        in_specs=[pl.BlockSpec((bm, k), row_blk),
                  _layer_spec((k, n), layer, lambda i: (0, 0), single_buffer=True),
                  pl.BlockSpec((bm, n), row_blk),
                  pl.BlockSpec((1, n), lambda i: (0, 0))],
        out_specs=out_specs,
        out_shape=out_shape,
        compiler_params=_params("parallel"),
        name="matmul_residual_norm",
    )(x, w, res, gain.reshape(1, n))
    return out[0] if final else out


MIX_PAIRS = 4
MIX_FIRST_PAIR = 1
(STEP_VU, STEP_GB, STEP_CX, STEP_QK) = range(MIX_PAIRS)


def _mix_in_kernel(n_ref, w_ref, sgun_ref, ws_ref, sb_ref, cw_ref,
                   qk_ref, v_ref, yb_ref, yc_ref,
                   u_ref, gv_ref, cb_ref, z_ref, carry_ref, *, blocks_per_seq):
    i, j = pl.program_id(0), pl.program_id(1)
    bm, width = u_ref.shape
    n_groups = width // HEAD_DIM
    n_chunks = bm // SGU_CHUNK
    first, second = slice(0, width), slice(width, 2 * width)

    def project(cols=slice(None)):
        return _dot(n_ref[...], w_ref[:, cols])

    @pl.when(j == STEP_VU)
    def _():
        v_ref[...] = project(first).astype(v_ref.dtype)
        u_ref[...] = _gelu(project(second))

    @pl.when(j == STEP_GB)
    def _():
        gv_ref[...] = _rms_rows(_gelu(project(first)), sgun_ref[...]).astype(gv_ref.dtype)
        cb_ref[...] = project(second)

    @pl.when(j == STEP_CX)
    def _():
        t_idx = lax.broadcasted_iota(jnp.int32, (SGU_CHUNK, SGU_CHUNK), 0)
        s_idx = lax.broadcasted_iota(jnp.int32, (SGU_CHUNK, SGU_CHUNK), 1)

        def sgu_groups(groups):
            for g in groups:
                cols = slice(g * HEAD_DIM, (g + 1) * HEAD_DIM)
                w_tril = jnp.where(s_idx <= t_idx, ws_ref[g], 0.0).astype(BF16)
                xg = jnp.concatenate(
                    [gv_ref[c * SGU_CHUNK:(c + 1) * SGU_CHUNK, cols] for c in range(n_chunks)], axis=1)
                mixed = _dot(w_tril, xg) + sb_ref[g]
                for c in range(n_chunks):
                    rows = slice(c * SGU_CHUNK, (c + 1) * SGU_CHUNK)
                    yb_ref[rows, cols] = (
                        u_ref[rows, cols] * mixed[:, c * HEAD_DIM:(c + 1) * HEAD_DIM]).astype(yb_ref.dtype)

        z_ref[...] = project(first)
        sgu_groups(range(n_groups // 2))
        z_ref[...] = z_ref[...] * project(second)
        sgu_groups(range(n_groups // 2, n_groups))

    @pl.when((j == STEP_CX) & (i % blocks_per_seq == 0))
    def _():
        carry_ref[...] = jnp.zeros_like(carry_ref)

    @pl.when(j == STEP_QK)
    def _():
        qk_ref[...] = project().astype(qk_ref.dtype)

    @pl.when(j > STEP_CX)
    def _():
        z = z_ref[...]
        prev1 = carry_ref[7:8, :]
        prev2 = carry_ref[6:7, :]
        row = lax.broadcasted_iota(jnp.int32, z.shape, 0)
        z1 = jnp.where(row == 0, prev1, pltpu.roll(z, 1, 0))
        z2 = jnp.where(row == 0, prev2, jnp.where(row == 1, prev1, pltpu.roll(z, 2, 0)))
        conv = cw_ref[0:1, :] * z2 + cw_ref[1:2, :] * z1 + cw_ref[2:3, :] * z
        yc_ref[...] = (cb_ref[...] * conv).astype(yc_ref.dtype)
        carry_ref[...] = z[bm - 8:bm, :]


def mixer_in(n, w, layer, sgu_norm, sgu_w_s, sgu_b, conv_w, *, seq, bm_target=512):
    m, d = n.shape
    width = d // 2
    assert w.shape[1] == d and w.shape[2] >= 2 * MIX_PAIRS * width
    bm = _pick(seq, bm_target)
    assert bm % SGU_CHUNK == 0 and bm >= 8
    n_groups = width // HEAD_DIM
    const2 = lambda i, j: (0, 0)
    const3 = lambda i, j: (0, 0, 0)
    row_blk = lambda i, j: (i, 0)
    return pl.pallas_call(
        functools.partial(_mix_in_kernel, blocks_per_seq=seq // bm),
        grid=(m // bm, MIX_PAIRS),
        in_specs=[pl.BlockSpec((bm, d), row_blk),
                  _layer_spec((d, 2 * width), layer, lambda i, j: (0, (j + MIX_FIRST_PAIR) % MIX_PAIRS)),
                  pl.BlockSpec((1, width), const2),
                  pl.BlockSpec((n_groups, SGU_CHUNK, SGU_CHUNK), const3),
                  pl.BlockSpec((n_groups, SGU_CHUNK, 1), const3),
                  pl.BlockSpec((CONV_TAPS, width), const2)],
        out_specs=[pl.BlockSpec((bm, 2 * width), row_blk),
                   pl.BlockSpec((bm, width), row_blk),
                   pl.BlockSpec((bm, width), row_blk),
                   pl.BlockSpec((bm, width), row_blk)],
        out_shape=[jax.ShapeDtypeStruct((m, 2 * width), BF16),
                   jax.ShapeDtypeStruct((m, width), BF16),
                   jax.ShapeDtypeStruct((m, width), BF16),
                   jax.ShapeDtypeStruct((m, width), BF16)],
        scratch_shapes=[pltpu.VMEM((bm, width), F32),
                        pltpu.VMEM((bm, width), BF16),
                        pltpu.VMEM((bm, width), F32),
                        pltpu.VMEM((bm, width), F32),
                        pltpu.VMEM((8, width), F32)],
        compiler_params=_params("arbitrary", "arbitrary"),
        name="mixer_in",
    )(n, w, sgu_norm.reshape(1, width), sgu_w_s, sgu_b.reshape(n_groups, SGU_CHUNK, 1), conv_w)


def _bf16_split3(x):
    p1 = x.astype(BF16).astype(F32)
    r1 = x - p1
    p2 = r1.astype(BF16).astype(F32)
    p3 = (r1 - p2).astype(BF16).astype(F32)
    return p1, p2, p3


BF16_SUBLANES = 16


def _diff_attn_kernel(scal_ref, q_ref, k_ref, v_ref, lam_ref, sub_ref, *rest, n_heads, tq, n_cast):
    cast_in, o_ref, cast_out = rest[:n_cast], rest[n_cast], rest[n_cast + 1:2 * n_cast + 1]
    kt_ref, vx_ref = rest[2 * n_cast + 1:]
    heads_here = kt_ref.shape[0]
    lam_init = scal_ref[n_heads]
    half = HEAD_DIM // 2
    seq = q_ref.shape[1]

    pos = lax.broadcasted_iota(jnp.int32, (HEAD_DIM, seq), 1)
    frow = lax.broadcasted_iota(jnp.int32, (HEAD_DIM, seq), 0)
    pos_hi = (pos - pos % HEAD_DIM).astype(F32)
    pos_lo = (pos % HEAD_DIM).astype(F32)
    pos_rows = jnp.where(frow < 3, pos_hi, jnp.where(frow < 6, pos_lo, 0.0)).astype(BF16)

    lp = lam_ref[...]
    lam = (jnp.exp(jnp.sum(lp[0:1] * lp[1:2], axis=-1, keepdims=True))
           - jnp.exp(jnp.sum(lp[2:3] * lp[3:4], axis=-1, keepdims=True)) + lam_init)
    sub = sub_ref[...]

    lane = lax.broadcasted_iota(jnp.int32, (tq, HEAD_DIM), 1)
    flane = lax.broadcasted_iota(jnp.int32, (2 * tq, HEAD_DIM), 1)
    r_idx = lax.broadcasted_iota(jnp.int32, (2 * tq, tq), 0)
    c_idx = lax.broadcasted_iota(jnp.int32, (2 * tq, tq), 1)
    causal = c_idx <= jnp.where(r_idx >= tq, r_idx - tq, r_idx)

    for hh in range(heads_here):
        cols = slice(hh * HEAD_DIM, (hh + 1) * HEAD_DIM)
        kt_ref[hh, :HEAD_DIM, :] = k_ref[0, :, cols].T
        kt_ref[hh, HEAD_DIM:, :] = pos_rows
        vx_ref[hh, :, :HEAD_DIM] = v_ref[0, :, cols]
        vx_ref[hh, :, HEAD_DIM:] = jnp.ones((seq, HEAD_DIM), BF16)
        slope = scal_ref[pl.program_id(1) * heads_here + hh]
        s1, s2, s3 = _bf16_split3(jnp.full((2 * tq, HEAD_DIM), slope, F32))
        piece = jnp.where(flane % 3 == 0, s1, jnp.where(flane % 3 == 1, s2, s3))
        q_feat = jnp.where(flane < 6, piece, 0.0).astype(BF16)

        for i in reversed(range(seq // tq)):
            start, stop = i * tq, (i + 1) * tq
            q = q_ref[0, start:stop, cols].astype(F32) * (1.0 / math.sqrt(half))
            qq = jnp.concatenate([jnp.where(lane < half, q, 0.0),
                                  jnp.where(lane >= half, q, 0.0)], axis=0).astype(BF16)
            t = _dot(jnp.concatenate([qq, q_feat], axis=1), kt_ref[hh, :, :stop])
            t_diag = jnp.where(causal, t[:, start:], NEG_BIG)
            m = jnp.max(t_diag, axis=-1, keepdims=True)
            if i:
                t_past = t[:, :start]
                m = jnp.maximum(m, jnp.max(t_past, axis=-1, keepdims=True))
                p = jnp.concatenate([jnp.exp(t_past - m), jnp.exp(t_diag - m)], axis=1)
            else:
                p = jnp.exp(t_diag - m)
            a = _dot(p.astype(BF16), vx_ref[hh, :stop, :])
            a = a[:, :HEAD_DIM] / a[:, HEAD_DIM:]
            o = a[:tq] - lam * a[tq:]
            o_ref[0, start:stop, cols] = (_rms_rows(o, sub) * (1.0 - lam_init)).astype(o_ref.dtype)

    for src, dst in zip(cast_in, cast_out):
        dst[...] = src[...].astype(dst.dtype)


def diff_attention(qk, v, lam_params, subln, lam_init, cast_stacks=(), cast_layer=0, *,
                   batch, seq, tq_target=256):
    width = v.shape[1]
    n_heads = width // HEAD_DIM
    tq = _pick(seq, tq_target)
    slopes = 2.0 ** (-8.0 * jnp.arange(1, n_heads + 1, dtype=F32) / n_heads)
    scal = jnp.concatenate([slopes, jnp.full((1,), lam_init, F32)])
    qk3 = qk.reshape(batch, seq, 2 * width)
    v3 = v.reshape(batch, seq, width)
    hps = _pick(n_heads, 2)
    groups = n_heads // hps
    head_blk = lambda off: pl.BlockSpec((1, seq, hps * HEAD_DIM), lambda b, h: (b, 0, off + h))

    steps = batch * groups
    cast_in_specs, cast_out_specs, cast_out_shapes = [], [], []
    for w in cast_stacks:
        _, rows, cols = w.shape
        n_blocks = steps
        while rows % n_blocks or (rows // n_blocks) % BF16_SUBLANES:
            n_blocks //= 2
        assert n_blocks >= 1 and steps % n_blocks == 0
        hold = steps // n_blocks
        blk = (None, rows // n_blocks, cols)
        cast_in_specs.append(pl.BlockSpec(blk, lambda b, h, hold=hold: (cast_layer, (b * groups + h) // hold, 0)))
        cast_out_specs.append(pl.BlockSpec(blk, lambda b, h, hold=hold: (0, (b * groups + h) // hold, 0)))
        cast_out_shapes.append(jax.ShapeDtypeStruct((1, rows, cols), BF16))

    out = pl.pallas_call(
        functools.partial(_diff_attn_kernel, n_heads=n_heads, tq=tq, n_cast=len(cast_stacks)),
        grid=(batch, groups),
        in_specs=[pl.BlockSpec(memory_space=pltpu.SMEM),
                  head_blk(0), head_blk(groups), head_blk(0),
                  pl.BlockSpec(lam_params.shape, lambda b, h: (0, 0)),
                  pl.BlockSpec((1, HEAD_DIM), lambda b, h: (0, 0))] + cast_in_specs,
        out_specs=[head_blk(0)] + cast_out_specs,
        out_shape=[jax.ShapeDtypeStruct((batch, seq, width), BF16)] + cast_out_shapes,
        scratch_shapes=[pltpu.VMEM((hps, 2 * HEAD_DIM, seq), BF16),
                        pltpu.VMEM((hps, seq, 2 * HEAD_DIM), BF16)],
        compiler_params=_params("arbitrary", "arbitrary"),
        name="diff_attention",
    )(scal, qk3, qk3, v3, lam_params, subln.reshape(1, HEAD_DIM), *cast_stacks)
    return out[0].reshape(batch * seq, width), list(out[1:])


def _merge_kernel(n_ref, wg0_ref, wg1_ref, wg2_ref, ya_ref, yb_ref, yc_ref,
                  wa_ref, wb_ref, wc_ref, o_ref):
    n = n_ref[...]

    def branch(wg_ref, y_ref, w_ref):
        return jax.nn.sigmoid(_dot(n, wg_ref[...])) * _dot(y_ref[...], w_ref[...])

    merged = branch(wg0_ref, ya_ref, wa_ref) + branch(wg1_ref, yb_ref, wb_ref)
    o_ref[...] = (merged + branch(wg2_ref, yc_ref, wc_ref)).astype(o_ref.dtype)


def gated_merge(n, w_mix_in, layer, y_a, y_b, y_c, w_a, w_b, w_c, *, bm_target=1024, bn_target=512):
    m, d = n.shape
    width = y_a.shape[1]
    bm, bn = _pick(m, bm_target), _pick(d, bn_target)
    nb = d // bn
    gate0 = (w_mix_in.shape[2] - N_BRANCHES * d) // bn
    row_blk = lambda i, j: (i, 0)
    col_blk = lambda i, j: (0, j)
    return pl.pallas_call(
        _merge_kernel,
        grid=(m // bm, nb),
        in_specs=[pl.BlockSpec((bm, d), row_blk),
                  _layer_spec((d, bn), layer, lambda i, j: (0, gate0 + j)),
                  _layer_spec((d, bn), layer, lambda i, j: (0, gate0 + nb + j)),
                  _layer_spec((d, bn), layer, lambda i, j: (0, gate0 + 2 * nb + j)),
                  pl.BlockSpec((bm, width), row_blk),
                  pl.BlockSpec((bm, width), row_blk),
                  pl.BlockSpec((bm, width), row_blk),
                  _layer_spec((width, bn), layer, col_blk),
                  _layer_spec((width, bn), layer, col_blk),
                  _layer_spec((width, bn), layer, col_blk)],
        out_specs=pl.BlockSpec((bm, bn), lambda i, j: (i, j)),
        out_shape=jax.ShapeDtypeStruct((m, d), BF16),
        compiler_params=_params("parallel", "arbitrary"),
        name="gated_merge",
    )(n, w_mix_in, w_mix_in, w_mix_in, y_a, y_b, y_c, w_a, w_b, w_c)


def _mem_attn_kernel(x_ref, n_ref, wq_ref, kv_ref, wo_ref, g_next_ref, o_ref, n_next_ref):
    mem_width = wq_ref.shape[1]
    q = _dot(n_ref[...], wq_ref[...]).astype(BF16)
    kv = kv_ref[0]
    scale = 1.0 / math.sqrt(HEAD_DIM)
    heads = []
    for hd in range(mem_width // HEAD_DIM):
        cols = slice(hd * HEAD_DIM, (hd + 1) * HEAD_DIM)
        k_h = kv[:, hd * HEAD_DIM:(hd + 1) * HEAD_DIM]
        v_h = kv[:, mem_width + hd * HEAD_DIM:mem_width + (hd + 1) * HEAD_DIM]
        s = _dot_t(q[:, cols], k_h) * scale
        p = jnp.exp(s - jnp.max(s, axis=-1, keepdims=True))
        denom = jnp.sum(p, axis=-1, keepdims=True)
        heads.append(_dot(p.astype(BF16), v_h) / denom)
    o = jnp.concatenate(heads, axis=1).astype(BF16)
    h = x_ref[...] + _dot(o, wo_ref[...])
    o_ref[...] = h
    n_next_ref[...] = _rms_rows(h, g_next_ref[...]).astype(n_next_ref.dtype)


def memory_attention(x, n, w_q, kv, w_o, layer, gain_next, *, seq, bm_target=512):
    m, d = x.shape
    mem_width = w_q.shape[2]
    bm = _pick(seq, bm_target)
    per_seq = seq // bm
    return pl.pallas_call(
        _mem_attn_kernel,
        grid=(m // bm,),
        in_specs=[pl.BlockSpec((bm, d), lambda i: (i, 0)),
                  pl.BlockSpec((bm, d), lambda i: (i, 0)),
                  _layer_spec((d, mem_width), layer, lambda i: (0, 0)),
                  pl.BlockSpec((1,) + kv.shape[1:], lambda i: (i // per_seq, 0, 0)),
                  _layer_spec((mem_width, d), layer, lambda i: (0, 0)),
                  pl.BlockSpec((1, d), lambda i: (0, 0))],
        out_specs=[pl.BlockSpec((bm, d), lambda i: (i, 0)),
                   pl.BlockSpec((bm, d), lambda i: (i, 0))],
        out_shape=[jax.ShapeDtypeStruct((m, d), F32),
                   jax.ShapeDtypeStruct((m, d), BF16)],
        compiler_params=_params("parallel"),
        name="memory_attention",
    )(x, n, w_q, kv, w_o, gain_next.reshape(1, d))


def kernel(x, mem, ffn1_norm, ffn1_w_in, ffn1_w_out, mix_norm, mix_w_in, diff_lambda, diff_subln, diff_w_out, sgu_norm, sgu_w_s, sgu_b, sgu_w_out, conv_w, conv_w_out, mix_w_o, xattn_norm, mem_norm, xattn_w_q, xattn_w_kv, xattn_w_o, ffn2_norm, ffn2_w_in, ffn2_w_out, final_norm):
    batch, seq, d = x.shape
    n_mem = mem.shape[1]
    depth = ffn1_norm.shape[0]
    h = x.reshape(batch * seq, d)
    mem2 = mem.reshape(batch * n_mem, d)

    big = dict(ffn1_w_in=ffn1_w_in, ffn1_w_out=ffn1_w_out, ffn2_w_in=ffn2_w_in, ffn2_w_out=ffn2_w_out,
               mix_w_in=mix_w_in, diff_w_out=diff_w_out, sgu_w_out=sgu_w_out, conv_w_out=conv_w_out,
               mix_w_o=mix_w_o)
    w = {name: stack[0:1].astype(BF16) for name, stack in big.items()}
    xattn_w_q, xattn_w_kv, xattn_w_o = (s.astype(BF16) for s in (xattn_w_q, xattn_w_kv, xattn_w_o))

    hidden = norm_swiglu_in(h, ffn1_norm[0], w["ffn1_w_in"], 0)
    for l in range(depth):
        lam_init = 0.8 - 0.6 * math.exp(-0.3 * l)
        last = l + 1 == depth

        h, n_mix = matmul_residual_norm(hidden, w["ffn1_w_out"], 0, h, 0.5, mix_norm[l])

        qk, v, y_b, y_c = mixer_in(n_mix, w["mix_w_in"], 0, sgu_norm[l], sgu_w_s[l], sgu_b[l], conv_w[l],
                                   seq=seq)
        y_a, casted = diff_attention(qk, v, diff_lambda[l], diff_subln[l], lam_init,
                                     () if last else tuple(big.values()), l + 1, batch=batch, seq=seq)
        merged = gated_merge(n_mix, w["mix_w_in"], 0, y_a, y_b, y_c,
                             w["diff_w_out"], w["sgu_w_out"], w["conv_w_out"])
        h, n_xattn = matmul_residual_norm(merged, w["mix_w_o"], 0, h, 1.0, xattn_norm[l], bm_target=512)

        kv = norm_matmul(mem2, mem_norm[l], xattn_w_kv, l).reshape(batch, n_mem, -1)
        h, n_ffn2 = memory_attention(h, n_xattn, xattn_w_q, kv, xattn_w_o, l, ffn2_norm[l], seq=seq)

        hidden = swiglu_in(n_ffn2, w["ffn2_w_in"], 0)
        if last:
            out = matmul_residual_norm(hidden, w["ffn2_w_out"], 0, h, 0.5, final_norm, final=True)
        else:
            h, n_ffn1 = matmul_residual_norm(hidden, w["ffn2_w_out"], 0, h, 0.5, ffn1_norm[l + 1])
            w = dict(zip(big, casted))
            hidden = swiglu_in(n_ffn1, w["ffn1_w_in"], 0)

    return out.reshape(batch, seq, d)
```

```python
import functools
import math

import jax
import jax.numpy as jnp
from jax import lax
from jax.experimental import pallas as pl
from jax.experimental.pallas import tpu as pltpu

HEAD_DIM = 128
SGU_CHUNK = 128
N_BRANCHES = 3
CONV_TAPS = 3
RMS_EPS = 1e-6
NEG_BIG = -1e30
V7X_VMEM_LIMIT_BYTES = 60 * 1024 * 1024
NORM_ROWS = 256

BF16 = jnp.bfloat16
F32 = jnp.float32


def _params(*sem):
    return pltpu.CompilerParams(dimension_semantics=sem, vmem_limit_bytes=V7X_VMEM_LIMIT_BYTES)


def _pick(total, target):
    b = min(total, target)
    while total % b:
        b //= 2
    return b


def _rms_rows(x, g):
    ms = jnp.mean(x * x, axis=-1, keepdims=True)
    return x * lax.rsqrt(ms + RMS_EPS) * g


def _fill_norm(x_ref, g_ref, n_ref):
    bm = x_ref.shape[0]
    rows = min(NORM_ROWS, bm)
    g = g_ref[...]

    def body(c, carry):
        r0 = pl.multiple_of(c * rows, rows)
        n_ref[pl.ds(r0, rows), :] = _rms_rows(x_ref[pl.ds(r0, rows), :], g).astype(n_ref.dtype)
        return carry

    lax.fori_loop(0, bm // rows, body, 0)


def _dot(a, b):
    return jnp.dot(a, b, preferred_element_type=F32)


def _dot_t(a, b):
    return lax.dot_general(a, b, (((1,), (1,)), ((), ())), preferred_element_type=F32)


def _gelu(x):
    return 0.5 * x * (1.0 + lax.erf(x * (1.0 / math.sqrt(2.0))))


def _norm_mm_kernel(x_ref, g_ref, w_ref, o_ref, n_ref):
    @pl.when(pl.program_id(1) == 0)
    def _():
        _fill_norm(x_ref, g_ref, n_ref)

    o_ref[...] = _dot(n_ref[...], w_ref[...]).astype(o_ref.dtype)


def _layer_spec(block, layer, index_map, *, single_buffer=False):
    return pl.BlockSpec((None,) + block, lambda *g: (layer,) + index_map(*g),
                        pipeline_mode=pl.Buffered(1) if single_buffer else None)


def norm_matmul(x, gain, w, layer, *, bm_target=1024, bn_target=1024):
    m, d = x.shape
    n = w.shape[2]
    bm, bn = _pick(m, bm_target), _pick(n, bn_target)
    return pl.pallas_call(
        _norm_mm_kernel,
        grid=(m // bm, n // bn),
        in_specs=[pl.BlockSpec((bm, d), lambda i, j: (i, 0)),
                  pl.BlockSpec((1, d), lambda i, j: (0, 0)),
                  _layer_spec((d, bn), layer, lambda i, j: (0, j))],
        out_specs=pl.BlockSpec((bm, bn), lambda i, j: (i, j)),
        out_shape=jax.ShapeDtypeStruct((m, n), BF16),
        scratch_shapes=[pltpu.VMEM((bm, d), BF16)],
        compiler_params=_params("parallel", "arbitrary"),
        name="norm_matmul",
    )(x, gain.reshape(1, d), w)


def _swiglu_kernel(n_ref, wa_ref, wb_ref, o_ref):
    n = n_ref[...]
    a = _dot(n, wa_ref[...])
    b = _dot(n, wb_ref[...])
    o_ref[...] = (a * jax.nn.sigmoid(a) * b).astype(o_ref.dtype)


def _norm_swiglu_kernel(x_ref, g_ref, wa_ref, wb_ref, o_ref, n_ref):
    @pl.when(pl.program_id(1) == 0)
    def _():
        _fill_norm(x_ref, g_ref, n_ref)

    _swiglu_kernel(n_ref, wa_ref, wb_ref, o_ref)


def norm_swiglu_in(x, gain, w_in, layer, *, bm_target=1024, bn_target=512):
    m, d = x.shape
    f = w_in.shape[2] // 2
    bm, bn = _pick(m, bm_target), _pick(f, bn_target)
    nb = f // bn
    return pl.pallas_call(
        _norm_swiglu_kernel,
        grid=(m // bm, nb),
        in_specs=[pl.BlockSpec((bm, d), lambda i, j: (i, 0)),
                  pl.BlockSpec((1, d), lambda i, j: (0, 0)),
                  _layer_spec((d, bn), layer, lambda i, j: (0, j)),
                  _layer_spec((d, bn), layer, lambda i, j: (0, j + nb))],
        out_specs=pl.BlockSpec((bm, bn), lambda i, j: (i, j)),
        out_shape=jax.ShapeDtypeStruct((m, f), BF16),
        scratch_shapes=[pltpu.VMEM((bm, d), BF16)],
        compiler_params=_params("parallel", "arbitrary"),
        name="norm_swiglu_in",
    )(x, gain.reshape(1, d), w_in, w_in)


def swiglu_in(n, w_in, layer, *, bm_target=1024, bn_target=512):
    m, d = n.shape
    f = w_in.shape[2] // 2
    bm, bn = _pick(m, bm_target), _pick(f, bn_target)
    nb = f // bn
    return pl.pallas_call(
        _swiglu_kernel,
        grid=(m // bm, nb),
        in_specs=[pl.BlockSpec((bm, d), lambda i, j: (i, 0)),
                  _layer_spec((d, bn), layer, lambda i, j: (0, j)),
                  _layer_spec((d, bn), layer, lambda i, j: (0, j + nb))],
        out_specs=pl.BlockSpec((bm, bn), lambda i, j: (i, j)),
        out_shape=jax.ShapeDtypeStruct((m, f), BF16),
        compiler_params=_params("parallel", "parallel"),
        name="swiglu_in",
    )(n, w_in, w_in)


def _mm_res_norm_kernel(x_ref, w_ref, r_ref, g_ref, *out_refs, scale, final):
    h = r_ref[...] + scale * _dot(x_ref[...], w_ref[...])
    normed = _rms_rows(h, g_ref[...])
    if final:
        out_refs[0][...] = normed
    else:
        out_refs[0][...] = h
        out_refs[1][...] = normed.astype(out_refs[1].dtype)


def matmul_residual_norm(x, w, layer, res, scale, gain, *, final=False, bm_target=512):
    m, k = x.shape
    n = w.shape[2]
    bm = _pick(m, bm_target)
    row_blk = lambda i: (i, 0)
    out_specs = [pl.BlockSpec((bm, n), row_blk)]
    out_shape = [jax.ShapeDtypeStruct((m, n), F32)]
    if not final:
        out_specs.append(pl.BlockSpec((bm, n), row_blk))
        out_shape.append(jax.ShapeDtypeStruct((m, n), BF16))
    out = pl.pallas_call(
        functools.partial(_mm_res_norm_kernel, scale=scale, final=final),
        grid=(m // bm,),
        in_specs=[pl.BlockSpec((bm, k), row_blk),
                  _layer_spec((k, n), layer, lambda i: (0, 0), single_buffer=True),
                  pl.BlockSpec((bm, n), row_blk),
                  pl.BlockSpec((1, n), lambda i: (0, 0))],
        out_specs=out_specs,
        out_shape=out_shape,
        compiler_params=_params("parallel"),
        name="matmul_residual_norm",
    )(x, w, res, gain.reshape(1, n))
    return out[0] if final else out


MIX_PAIRS = 4
MIX_FIRST_PAIR = 1
(STEP_VU, STEP_GB, STEP_CX, STEP_QK) = range(MIX_PAIRS)


def _mix_in_kernel(n_ref, w_ref, sgun_ref, ws_ref, sb_ref, cw_ref,
                   qk_ref, v_ref, yb_ref, yc_ref,
                   u_ref, gv_ref, cb_ref, z_ref, carry_ref, *, blocks_per_seq):
    i, j = pl.program_id(0), pl.program_id(1)
    bm, width = u_ref.shape
    n_groups = width // HEAD_DIM
    n_chunks = bm // SGU_CHUNK
    first, second = slice(0, width), slice(width, 2 * width)

    def project(cols=slice(None)):
        return _dot(n_ref[...], w_ref[:, cols])

    @pl.when(j == STEP_VU)
    def _():
        v_ref[...] = project(first).astype(v_ref.dtype)
        u_ref[...] = _gelu(project(second))

    @pl.when(j == STEP_GB)
    def _():
        gv_ref[...] = _rms_rows(_gelu(project(first)), sgun_ref[...]).astype(gv_ref.dtype)
        cb_ref[...] = project(second)

    @pl.when(j == STEP_CX)
    def _():
        t_idx = lax.broadcasted_iota(jnp.int32, (SGU_CHUNK, SGU_CHUNK), 0)
        s_idx = lax.broadcasted_iota(jnp.int32, (SGU_CHUNK, SGU_CHUNK), 1)

        def sgu_groups(groups):
            for g in groups:
                cols = slice(g * HEAD_DIM, (g + 1) * HEAD_DIM)
                w_tril = jnp.where(s_idx <= t_idx, ws_ref[g], 0.0).astype(BF16)
                xg = jnp.concatenate(
                    [gv_ref[c * SGU_CHUNK:(c + 1) * SGU_CHUNK, cols] for c in range(n_chunks)], axis=1)
                mixed = _dot(w_tril, xg) + sb_ref[g]
                for c in range(n_chunks):
                    rows = slice(c * SGU_CHUNK, (c + 1) * SGU_CHUNK)
                    yb_ref[rows, cols] = (
                        u_ref[rows, cols] * mixed[:, c * HEAD_DIM:(c + 1) * HEAD_DIM]).astype(yb_ref.dtype)

        z_ref[...] = project(first)
        sgu_groups(range(n_groups // 2))
        z_ref[...] = z_ref[...] * project(second)
        sgu_groups(range(n_groups // 2, n_groups))

    @pl.when((j == STEP_CX) & (i % blocks_per_seq == 0))
    def _():
        carry_ref[...] = jnp.zeros_like(carry_ref)

    @pl.when(j == STEP_QK)
    def _():
        qk_ref[...] = project().astype(qk_ref.dtype)

    @pl.when(j > STEP_CX)
    def _():
        z = z_ref[...]
        prev1 = carry_ref[7:8, :]
        prev2 = carry_ref[6:7, :]
        row = lax.broadcasted_iota(jnp.int32, z.shape, 0)
        z1 = jnp.where(row == 0, prev1, pltpu.roll(z, 1, 0))
        z2 = jnp.where(row == 0, prev2, jnp.where(row == 1, prev1, pltpu.roll(z, 2, 0)))
        conv = cw_ref[0:1, :] * z2 + cw_ref[1:2, :] * z1 + cw_ref[2:3, :] * z
        yc_ref[...] = (cb_ref[...] * conv).astype(yc_ref.dtype)
        carry_ref[...] = z[bm - 8:bm, :]


def mixer_in(n, w, layer, sgu_norm, sgu_w_s, sgu_b, conv_w, *, seq, bm_target=512):
    m, d = n.shape
    width = d // 2
    assert w.shape[1] == d and w.shape[2] >= 2 * MIX_PAIRS * width
    bm = _pick(seq, bm_target)
    assert bm % SGU_CHUNK == 0 and bm >= 8
    n_groups = width // HEAD_DIM
    const2 = lambda i, j: (0, 0)
    const3 = lambda i, j: (0, 0, 0)
    row_blk = lambda i, j: (i, 0)
    return pl.pallas_call(
        functools.partial(_mix_in_kernel, blocks_per_seq=seq // bm),
        grid=(m // bm, MIX_PAIRS),
        in_specs=[pl.BlockSpec((bm, d), row_blk),
                  _layer_spec((d, 2 * width), layer, lambda i, j: (0, (j + MIX_FIRST_PAIR) % MIX_PAIRS)),
                  pl.BlockSpec((1, width), const2),
                  pl.BlockSpec((n_groups, SGU_CHUNK, SGU_CHUNK), const3),
                  pl.BlockSpec((n_groups, SGU_CHUNK, 1), const3),
                  pl.BlockSpec((CONV_TAPS, width), const2)],
        out_specs=[pl.BlockSpec((bm, 2 * width), row_blk),
                   pl.BlockSpec((bm, width), row_blk),
                   pl.BlockSpec((bm, width), row_blk),
                   pl.BlockSpec((bm, width), row_blk)],
        out_shape=[jax.ShapeDtypeStruct((m, 2 * width), BF16),
                   jax.ShapeDtypeStruct((m, width), BF16),
                   jax.ShapeDtypeStruct((m, width), BF16),
                   jax.ShapeDtypeStruct((m, width), BF16)],
        scratch_shapes=[pltpu.VMEM((bm, width), F32),
                        pltpu.VMEM((bm, width), BF16),
                        pltpu.VMEM((bm, width), F32),
                        pltpu.VMEM((bm, width), F32),
                        pltpu.VMEM((8, width), F32)],
        compiler_params=_params("arbitrary", "arbitrary"),
        name="mixer_in",
    )(n, w, sgu_norm.reshape(1, width), sgu_w_s, sgu_b.reshape(n_groups, SGU_CHUNK, 1), conv_w)


def _bf16_split3(x):
    p1 = x.astype(BF16).astype(F32)
    r1 = x - p1
    p2 = r1.astype(BF16).astype(F32)
    p3 = (r1 - p2).astype(BF16).astype(F32)
    return p1, p2, p3


BF16_SUBLANES = 16


def _diff_attn_kernel(scal_ref, q_ref, k_ref, v_ref, lam_ref, sub_ref, *rest, n_heads, tq, n_cast):
    cast_in, o_ref, cast_out = rest[:n_cast], rest[n_cast], rest[n_cast + 1:2 * n_cast + 1]
    kt_ref, vx_ref = rest[2 * n_cast + 1:]
    heads_here = kt_ref.shape[0]
    lam_init = scal_ref[n_heads]
    half = HEAD_DIM // 2
    seq = q_ref.shape[1]

    pos = lax.broadcasted_iota(jnp.int32, (HEAD_DIM, seq), 1)
    frow = lax.broadcasted_iota(jnp.int32, (HEAD_DIM, seq), 0)
    pos_hi = (pos - pos % HEAD_DIM).astype(F32)
    pos_lo = (pos % HEAD_DIM).astype(F32)
    pos_rows = jnp.where(frow < 3, pos_hi, jnp.where(frow < 6, pos_lo, 0.0)).astype(BF16)

    lp = lam_ref[...]
    lam = (jnp.exp(jnp.sum(lp[0:1] * lp[1:2], axis=-1, keepdims=True))
           - jnp.exp(jnp.sum(lp[2:3] * lp[3:4], axis=-1, keepdims=True)) + lam_init)
    sub = sub_ref[...]

    lane = lax.broadcasted_iota(jnp.int32, (tq, HEAD_DIM), 1)
    flane = lax.broadcasted_iota(jnp.int32, (2 * tq, HEAD_DIM), 1)
    r_idx = lax.broadcasted_iota(jnp.int32, (2 * tq, tq), 0)
    c_idx = lax.broadcasted_iota(jnp.int32, (2 * tq, tq), 1)
    causal = c_idx <= jnp.where(r_idx >= tq, r_idx - tq, r_idx)

    for hh in range(heads_here):
        cols = slice(hh * HEAD_DIM, (hh + 1) * HEAD_DIM)
        kt_ref[hh, :HEAD_DIM, :] = k_ref[0, :, cols].T
        kt_ref[hh, HEAD_DIM:, :] = pos_rows
        vx_ref[hh, :, :HEAD_DIM] = v_ref[0, :, cols]
        vx_ref[hh, :, HEAD_DIM:] = jnp.ones((seq, HEAD_DIM), BF16)
        slope = scal_ref[pl.program_id(1) * heads_here + hh]
        s1, s2, s3 = _bf16_split3(jnp.full((2 * tq, HEAD_DIM), slope, F32))
        piece = jnp.where(flane % 3 == 0, s1, jnp.where(flane % 3 == 1, s2, s3))
        q_feat = jnp.where(flane < 6, piece, 0.0).astype(BF16)

        for i in reversed(range(seq // tq)):
            start, stop = i * tq, (i + 1) * tq
            q = q_ref[0, start:stop, cols].astype(F32) * (1.0 / math.sqrt(half))
            qq = jnp.concatenate([jnp.where(lane < half, q, 0.0),
                                  jnp.where(lane >= half, q, 0.0)], axis=0).astype(BF16)
            t = _dot(jnp.concatenate([qq, q_feat], axis=1), kt_ref[hh, :, :stop])
            t_diag = jnp.where(causal, t[:, start:], NEG_BIG)
            m = jnp.max(t_diag, axis=-1, keepdims=True)
            if i:
                t_past = t[:, :start]
                m = jnp.maximum(m, jnp.max(t_past, axis=-1, keepdims=True))
                p = jnp.concatenate([jnp.exp(t_past - m), jnp.exp(t_diag - m)], axis=1)
            else:
                p = jnp.exp(t_diag - m)
            a = _dot(p.astype(BF16), vx_ref[hh, :stop, :])
            a = a[:, :HEAD_DIM] / a[:, HEAD_DIM:]
            o = a[:tq] - lam * a[tq:]
            o_ref[0, start:stop, cols] = (_rms_rows(o, sub) * (1.0 - lam_init)).astype(o_ref.dtype)

    for src, dst in zip(cast_in, cast_out):
        dst[...] = src[...].astype(dst.dtype)


def diff_attention(qk, v, lam_params, subln, lam_init, cast_stacks=(), cast_layer=0, *,
                   batch, seq, tq_target=256):
    width = v.shape[1]
    n_heads = width // HEAD_DIM
    tq = _pick(seq, tq_target)
    slopes = 2.0 ** (-8.0 * jnp.arange(1, n_heads + 1, dtype=F32) / n_heads)
    scal = jnp.concatenate([slopes, jnp.full((1,), lam_init, F32)])
    qk3 = qk.reshape(batch, seq, 2 * width)
    v3 = v.reshape(batch, seq, width)
    hps = _pick(n_heads, 2)
    groups = n_heads // hps
    head_blk = lambda off: pl.BlockSpec((1, seq, hps * HEAD_DIM), lambda b, h: (b, 0, off + h))

    steps = batch * groups
    cast_in_specs, cast_out_specs, cast_out_shapes = [], [], []
    for w in cast_stacks:
        _, rows, cols = w.shape
        n_blocks = steps
        while rows % n_blocks or (rows // n_blocks) % BF16_SUBLANES:
            n_blocks //= 2
        assert n_blocks >= 1 and steps % n_blocks == 0
        hold = steps // n_blocks
        blk = (None, rows // n_blocks, cols)
        cast_in_specs.append(pl.BlockSpec(blk, lambda b, h, hold=hold: (cast_layer, (b * groups + h) // hold, 0)))
        cast_out_specs.append(pl.BlockSpec(blk, lambda b, h, hold=hold: (0, (b * groups + h) // hold, 0)))
        cast_out_shapes.append(jax.ShapeDtypeStruct((1, rows, cols), BF16))

    out = pl.pallas_call(
        functools.partial(_diff_attn_kernel, n_heads=n_heads, tq=tq, n_cast=len(cast_stacks)),
        grid=(batch, groups),
        in_specs=[pl.BlockSpec(memory_space=pltpu.SMEM),
                  head_blk(0), head_blk(groups), head_blk(0),
                  pl.BlockSpec(lam_params.shape, lambda b, h: (0, 0)),
                  pl.BlockSpec((1, HEAD_DIM), lambda b, h: (0, 0))] + cast_in_specs,
        out_specs=[head_blk(0)] + cast_out_specs,
        out_shape=[jax.ShapeDtypeStruct((batch, seq, width), BF16)] + cast_out_shapes,
        scratch_shapes=[pltpu.VMEM((hps, 2 * HEAD_DIM, seq), BF16),
                        pltpu.VMEM((hps, seq, 2 * HEAD_DIM), BF16)],
        compiler_params=_params("arbitrary", "arbitrary"),
        name="diff_attention",
    )(scal, qk3, qk3, v3, lam_params, subln.reshape(1, HEAD_DIM), *cast_stacks)
    return out[0].reshape(batch * seq, width), list(out[1:])


def _merge_kernel(n_ref, wg0_ref, wg1_ref, wg2_ref, ya_ref, yb_ref, yc_ref,
                  wa_ref, wb_ref, wc_ref, o_ref):
    n = n_ref[...]

    def branch(wg_ref, y_ref, w_ref):
        return jax.nn.sigmoid(_dot(n, wg_ref[...])) * _dot(y_ref[...], w_ref[...])

    merged = branch(wg0_ref, ya_ref, wa_ref) + branch(wg1_ref, yb_ref, wb_ref)
    o_ref[...] = (merged + branch(wg2_ref, yc_ref, wc_ref)).astype(o_ref.dtype)


def gated_merge(n, w_mix_in, layer, y_a, y_b, y_c, w_a, w_b, w_c, *, bm_target=1024, bn_target=512):
    m, d = n.shape
    width = y_a.shape[1]
    bm, bn = _pick(m, bm_target), _pick(d, bn_target)
    nb = d // bn
    gate0 = (w_mix_in.shape[2] - N_BRANCHES * d) // bn
    row_blk = lambda i, j: (i, 0)
    col_blk = lambda i, j: (0, j)
    return pl.pallas_call(
        _merge_kernel,
        grid=(m // bm, nb),
        in_specs=[pl.BlockSpec((bm, d), row_blk),
                  _layer_spec((d, bn), layer, lambda i, j: (0, gate0 + j)),
                  _layer_spec((d, bn), layer, lambda i, j: (0, gate0 + nb + j)),
                  _layer_spec((d, bn), layer, lambda i, j: (0, gate0 + 2 * nb + j)),
                  pl.BlockSpec((bm, width), row_blk),
                  pl.BlockSpec((bm, width), row_blk),
                  pl.BlockSpec((bm, width), row_blk),
                  _layer_spec((width, bn), layer, col_blk),
                  _layer_spec((width, bn), layer, col_blk),
                  _layer_spec((width, bn), layer, col_blk)],
        out_specs=pl.BlockSpec((bm, bn), lambda i, j: (i, j)),
        out_shape=jax.ShapeDtypeStruct((m, d), BF16),
        compiler_params=_params("parallel", "arbitrary"),
        name="gated_merge",
    )(n, w_mix_in, w_mix_in, w_mix_in, y_a, y_b, y_c, w_a, w_b, w_c)


def _mem_attn_kernel(x_ref, merged_ref, wmix_ref, g_ref, wq_ref, kv_ref, wo_ref, g_next_ref,
                     o_ref, n_next_ref):
    mem_width = wq_ref.shape[1]
    x = x_ref[...] + _dot(merged_ref[...], wmix_ref[...])
    n = _rms_rows(x, g_ref[...]).astype(BF16)
    q = _dot(n, wq_ref[...]).astype(BF16)
    kv = kv_ref[0]
    scale = 1.0 / math.sqrt(HEAD_DIM)
    heads = []
    for hd in range(mem_width // HEAD_DIM):
        cols = slice(hd * HEAD_DIM, (hd + 1) * HEAD_DIM)
        k_h = kv[:, hd * HEAD_DIM:(hd + 1) * HEAD_DIM]
        v_h = kv[:, mem_width + hd * HEAD_DIM:mem_width + (hd + 1) * HEAD_DIM]
        s = _dot_t(q[:, cols], k_h) * scale
        p = jnp.exp(s - jnp.max(s, axis=-1, keepdims=True))
        denom = jnp.sum(p, axis=-1, keepdims=True)
        heads.append(_dot(p.astype(BF16), v_h) / denom)
    o = jnp.concatenate(heads, axis=1).astype(BF16)
    h = x + _dot(o, wo_ref[...])
    o_ref[...] = h
    n_next_ref[...] = _rms_rows(h, g_next_ref[...]).astype(n_next_ref.dtype)


def mixer_out_memory_attention(x, merged, w_mix_o, mix_layer, gain, w_q, kv, w_o, layer, gain_next,
                               *, seq, bm_target=512):
    m, d = x.shape
    mem_width = w_q.shape[2]
    bm = _pick(seq, bm_target)
    per_seq = seq // bm
    return pl.pallas_call(
        _mem_attn_kernel,
        grid=(m // bm,),
        in_specs=[pl.BlockSpec((bm, d), lambda i: (i, 0)),
                  pl.BlockSpec((bm, d), lambda i: (i, 0)),
                  _layer_spec((d, d), mix_layer, lambda i: (0, 0), single_buffer=True),
                  pl.BlockSpec((1, d), lambda i: (0, 0)),
                  _layer_spec((d, mem_width), layer, lambda i: (0, 0)),
                  pl.BlockSpec((1,) + kv.shape[1:], lambda i: (i // per_seq, 0, 0)),
                  _layer_spec((mem_width, d), layer, lambda i: (0, 0)),
                  pl.BlockSpec((1, d), lambda i: (0, 0))],
        out_specs=[pl.BlockSpec((bm, d), lambda i: (i, 0)),
                   pl.BlockSpec((bm, d), lambda i: (i, 0))],
        out_shape=[jax.ShapeDtypeStruct((m, d), F32),
                   jax.ShapeDtypeStruct((m, d), BF16)],
        compiler_params=_params("parallel"),
        name="memory_attention",
    )(x, merged, w_mix_o, gain.reshape(1, d), w_q, kv, w_o, gain_next.reshape(1, d))


def kernel(x, mem, ffn1_norm, ffn1_w_in, ffn1_w_out, mix_norm, mix_w_in, diff_lambda, diff_subln, diff_w_out, sgu_norm, sgu_w_s, sgu_b, sgu_w_out, conv_w, conv_w_out, mix_w_o, xattn_norm, mem_norm, xattn_w_q, xattn_w_kv, xattn_w_o, ffn2_norm, ffn2_w_in, ffn2_w_out, final_norm):
    batch, seq, d = x.shape
    n_mem = mem.shape[1]
    depth = ffn1_norm.shape[0]
    h = x.reshape(batch * seq, d)
    mem2 = mem.reshape(batch * n_mem, d)

    big = dict(ffn1_w_in=ffn1_w_in, ffn1_w_out=ffn1_w_out, ffn2_w_in=ffn2_w_in, ffn2_w_out=ffn2_w_out,
               mix_w_in=mix_w_in, diff_w_out=diff_w_out, sgu_w_out=sgu_w_out, conv_w_out=conv_w_out,
               mix_w_o=mix_w_o)
    w = {name: stack[0:1].astype(BF16) for name, stack in big.items()}
    xattn_w_q, xattn_w_kv, xattn_w_o = (s.astype(BF16) for s in (xattn_w_q, xattn_w_kv, xattn_w_o))

    hidden = norm_swiglu_in(h, ffn1_norm[0], w["ffn1_w_in"], 0)
    for l in range(depth):
        lam_init = 0.8 - 0.6 * math.exp(-0.3 * l)
        last = l + 1 == depth

        h, n_mix = matmul_residual_norm(hidden, w["ffn1_w_out"], 0, h, 0.5, mix_norm[l])

        qk, v, y_b, y_c = mixer_in(n_mix, w["mix_w_in"], 0, sgu_norm[l], sgu_w_s[l], sgu_b[l], conv_w[l],
                                   seq=seq)
        y_a, casted = diff_attention(qk, v, diff_lambda[l], diff_subln[l], lam_init,
                                     () if last else tuple(big.values()), l + 1, batch=batch, seq=seq)
        merged = gated_merge(n_mix, w["mix_w_in"], 0, y_a, y_b, y_c,
                             w["diff_w_out"], w["sgu_w_out"], w["conv_w_out"])
        kv = norm_matmul(mem2, mem_norm[l], xattn_w_kv, l).reshape(batch, n_mem, -1)
        h, n_ffn2 = mixer_out_memory_attention(h, merged, w["mix_w_o"], 0, xattn_norm[l],
                                               xattn_w_q, kv, xattn_w_o, l, ffn2_norm[l], seq=seq)

        hidden = swiglu_in(n_ffn2, w["ffn2_w_in"], 0)
        if last:
            out = matmul_residual_norm(hidden, w["ffn2_w_out"], 0, h, 0.5, final_norm, final=True)
        else:
            h, n_ffn1 = matmul_residual_norm(hidden, w["ffn2_w_out"], 0, h, 0.5, ffn1_norm[l + 1])
            w = dict(zip(big, casted))
            hidden = swiglu_in(n_ffn1, w["ffn1_w_in"], 0)

    return out.reshape(batch, seq, d)
```

```python
import functools
import math

import jax
import jax.numpy as jnp
from jax import lax
from jax.experimental import pallas as pl
from jax.experimental.pallas import tpu as pltpu

HEAD_DIM = 128
SGU_CHUNK = 128
N_BRANCHES = 3
CONV_TAPS = 3
RMS_EPS = 1e-6
NEG_BIG = -1e30
V7X_VMEM_LIMIT_BYTES = 60 * 1024 * 1024
NORM_ROWS = 256

BF16 = jnp.bfloat16
F32 = jnp.float32


def _params(*sem):
    return pltpu.CompilerParams(dimension_semantics=sem, vmem_limit_bytes=V7X_VMEM_LIMIT_BYTES)


def _pick(total, target):
    b = min(total, target)
    while total % b:
        b //= 2
    return b


def _rms_rows(x, g):
    ms = jnp.mean(x * x, axis=-1, keepdims=True)
    return x * lax.rsqrt(ms + RMS_EPS) * g


def _fill_norm(x_ref, g_ref, n_ref):
    bm = x_ref.shape[0]
    rows = min(NORM_ROWS, bm)
    g = g_ref[...]

    def body(c, carry):
        r0 = pl.multiple_of(c * rows, rows)
        n_ref[pl.ds(r0, rows), :] = _rms_rows(x_ref[pl.ds(r0, rows), :], g).astype(n_ref.dtype)
        return carry

    lax.fori_loop(0, bm // rows, body, 0)


def _dot(a, b):
    return jnp.dot(a, b, preferred_element_type=F32)


def _dot_t(a, b):
    return lax.dot_general(a, b, (((1,), (1,)), ((), ())), preferred_element_type=F32)


def _gelu(x):
    return 0.5 * x * (1.0 + lax.erf(x * (1.0 / math.sqrt(2.0))))


def _norm_mm_kernel(x_ref, g_ref, w_ref, o_ref, n_ref):
    @pl.when(pl.program_id(1) == 0)
    def _():
        _fill_norm(x_ref, g_ref, n_ref)

    o_ref[...] = _dot(n_ref[...], w_ref[...]).astype(o_ref.dtype)


def _layer_spec(block, layer, index_map, *, single_buffer=False):
    return pl.BlockSpec((None,) + block, lambda *g: (layer,) + index_map(*g),
                        pipeline_mode=pl.Buffered(1) if single_buffer else None)


def norm_matmul(x, gain, w, layer, *, bm_target=1024, bn_target=1024):
    m, d = x.shape
    n = w.shape[2]
    bm, bn = _pick(m, bm_target), _pick(n, bn_target)
    return pl.pallas_call(
        _norm_mm_kernel,
        grid=(m // bm, n // bn),
        in_specs=[pl.BlockSpec((bm, d), lambda i, j: (i, 0)),
                  pl.BlockSpec((1, d), lambda i, j: (0, 0)),
                  _layer_spec((d, bn), layer, lambda i, j: (0, j))],
        out_specs=pl.BlockSpec((bm, bn), lambda i, j: (i, j)),
        out_shape=jax.ShapeDtypeStruct((m, n), BF16),
        scratch_shapes=[pltpu.VMEM((bm, d), BF16)],
        compiler_params=_params("parallel", "arbitrary"),
        name="norm_matmul",
    )(x, gain.reshape(1, d), w)


def _swiglu_kernel(n_ref, wa_ref, wb_ref, o_ref):
    n = n_ref[...]
    a = _dot(n, wa_ref[...])
    b = _dot(n, wb_ref[...])
    o_ref[...] = (a * jax.nn.sigmoid(a) * b).astype(o_ref.dtype)


def _norm_swiglu_kernel(x_ref, g_ref, wa_ref, wb_ref, o_ref, n_ref):
    @pl.when(pl.program_id(1) == 0)
    def _():
        _fill_norm(x_ref, g_ref, n_ref)

    _swiglu_kernel(n_ref, wa_ref, wb_ref, o_ref)


def norm_swiglu_in(x, gain, w_in, layer, *, bm_target=1024, bn_target=512):
    m, d = x.shape
    f = w_in.shape[2] // 2
    bm, bn = _pick(m, bm_target), _pick(f, bn_target)
    nb = f // bn
    return pl.pallas_call(
        _norm_swiglu_kernel,
        grid=(m // bm, nb),
        in_specs=[pl.BlockSpec((bm, d), lambda i, j: (i, 0)),
                  pl.BlockSpec((1, d), lambda i, j: (0, 0)),
                  _layer_spec((d, bn), layer, lambda i, j: (0, j)),
                  _layer_spec((d, bn), layer, lambda i, j: (0, j + nb))],
        out_specs=pl.BlockSpec((bm, bn), lambda i, j: (i, j)),
        out_shape=jax.ShapeDtypeStruct((m, f), BF16),
        scratch_shapes=[pltpu.VMEM((bm, d), BF16)],
        compiler_params=_params("parallel", "arbitrary"),
        name="norm_swiglu_in",
    )(x, gain.reshape(1, d), w_in, w_in)


def swiglu_in(n, w_in, layer, *, bm_target=1024, bn_target=512):
    m, d = n.shape
    f = w_in.shape[2] // 2
    bm, bn = _pick(m, bm_target), _pick(f, bn_target)
    nb = f // bn
    return pl.pallas_call(
        _swiglu_kernel,
        grid=(m // bm, nb),
        in_specs=[pl.BlockSpec((bm, d), lambda i, j: (i, 0)),
                  _layer_spec((d, bn), layer, lambda i, j: (0, j)),
                  _layer_spec((d, bn), layer, lambda i, j: (0, j + nb))],
        out_specs=pl.BlockSpec((bm, bn), lambda i, j: (i, j)),
        out_shape=jax.ShapeDtypeStruct((m, f), BF16),
        compiler_params=_params("parallel", "parallel"),
        name="swiglu_in",
    )(n, w_in, w_in)


def _mm_res_norm_kernel(x_ref, w_ref, r_ref, g_ref, *out_refs, scale, final):
    h = r_ref[...] + scale * _dot(x_ref[...], w_ref[...])
    normed = _rms_rows(h, g_ref[...])
    if final:
        out_refs[0][...] = normed
    else:
        out_refs[0][...] = h
        out_refs[1][...] = normed.astype(out_refs[1].dtype)


def matmul_residual_norm(x, w, layer, res, scale, gain, *, final=False, bm_target=512):
    m, k = x.shape
    n = w.shape[2]
    bm = _pick(m, bm_target)
    row_blk = lambda i: (i, 0)
    out_specs = [pl.BlockSpec((bm, n), row_blk)]
    out_shape = [jax.ShapeDtypeStruct((m, n), F32)]
    if not final:
        out_specs.append(pl.BlockSpec((bm, n), row_blk))
        out_shape.append(jax.ShapeDtypeStruct((m, n), BF16))
    out = pl.pallas_call(
        functools.partial(_mm_res_norm_kernel, scale=scale, final=final),
        grid=(m // bm,),
        in_specs=[pl.BlockSpec((bm, k), row_blk),
                  _layer_spec((k, n), layer, lambda i: (0, 0), single_buffer=True),
                  pl.BlockSpec((bm, n), row_blk),
                  pl.BlockSpec((1, n), lambda i: (0, 0))],
        out_specs=out_specs,
        out_shape=out_shape,
        compiler_params=_params("parallel"),
        name="matmul_residual_norm",
    )(x, w, res, gain.reshape(1, n))
    return out[0] if final else out


MIX_PAIRS = 4
MIX_FIRST_PAIR = 1
(STEP_VU, STEP_GB, STEP_CX, STEP_QK) = range(MIX_PAIRS)


def _mix_in_kernel(n_ref, w_ref, sgun_ref, ws_ref, sb_ref, cw_ref,
                   qk_ref, v_ref, yb_ref, yc_ref,
                   u_ref, gv_ref, cb_ref, z_ref, carry_ref, *, blocks_per_seq):
    i, j = pl.program_id(0), pl.program_id(1)
    bm, width = u_ref.shape
    n_groups = width // HEAD_DIM
    n_chunks = bm // SGU_CHUNK
    first, second = slice(0, width), slice(width, 2 * width)

    def project(cols=slice(None)):
        return _dot(n_ref[...], w_ref[:, cols])

    @pl.when(j == STEP_VU)
    def _():
        v_ref[...] = project(first).astype(v_ref.dtype)
        u_ref[...] = _gelu(project(second))

    @pl.when(j == STEP_GB)
    def _():
        gv_ref[...] = _rms_rows(_gelu(project(first)), sgun_ref[...]).astype(gv_ref.dtype)
        cb_ref[...] = project(second)

    @pl.when(j == STEP_CX)
    def _():
        t_idx = lax.broadcasted_iota(jnp.int32, (SGU_CHUNK, SGU_CHUNK), 0)
        s_idx = lax.broadcasted_iota(jnp.int32, (SGU_CHUNK, SGU_CHUNK), 1)

        def sgu_groups(groups):
            for g in groups:
                cols = slice(g * HEAD_DIM, (g + 1) * HEAD_DIM)
                w_tril = jnp.where(s_idx <= t_idx, ws_ref[g], 0.0).astype(BF16)
                xg = jnp.concatenate(
                    [gv_ref[c * SGU_CHUNK:(c + 1) * SGU_CHUNK, cols] for c in range(n_chunks)], axis=1)
                mixed = _dot(w_tril, xg) + sb_ref[g]
                for c in range(n_chunks):
                    rows = slice(c * SGU_CHUNK, (c + 1) * SGU_CHUNK)
                    yb_ref[rows, cols] = (
                        u_ref[rows, cols] * mixed[:, c * HEAD_DIM:(c + 1) * HEAD_DIM]).astype(yb_ref.dtype)

        z_ref[...] = project(first)
        sgu_groups(range(n_groups // 2))
        z_ref[...] = z_ref[...] * project(second)
        sgu_groups(range(n_groups // 2, n_groups))

    @pl.when((j == STEP_CX) & (i % blocks_per_seq == 0))
    def _():
        carry_ref[...] = jnp.zeros_like(carry_ref)

    @pl.when(j == STEP_QK)
    def _():
        qk_ref[...] = project().astype(qk_ref.dtype)

    @pl.when(j > STEP_CX)
    def _():
        z = z_ref[...]
        prev1 = carry_ref[7:8, :]
        prev2 = carry_ref[6:7, :]
        row = lax.broadcasted_iota(jnp.int32, z.shape, 0)
        z1 = jnp.where(row == 0, prev1, pltpu.roll(z, 1, 0))
        z2 = jnp.where(row == 0, prev2, jnp.where(row == 1, prev1, pltpu.roll(z, 2, 0)))
        conv = cw_ref[0:1, :] * z2 + cw_ref[1:2, :] * z1 + cw_ref[2:3, :] * z
        yc_ref[...] = (cb_ref[...] * conv).astype(yc_ref.dtype)
        carry_ref[...] = z[bm - 8:bm, :]


def mixer_in(n, w, layer, sgu_norm, sgu_w_s, sgu_b, conv_w, *, seq, bm_target=512):
    m, d = n.shape
    width = d // 2
    assert w.shape[1] == d and w.shape[2] >= 2 * MIX_PAIRS * width
    bm = _pick(seq, bm_target)
    assert bm % SGU_CHUNK == 0 and bm >= 8
    n_groups = width // HEAD_DIM
    const2 = lambda i, j: (0, 0)
    const3 = lambda i, j: (0, 0, 0)
    row_blk = lambda i, j: (i, 0)
    return pl.pallas_call(
        functools.partial(_mix_in_kernel, blocks_per_seq=seq // bm),
        grid=(m // bm, MIX_PAIRS),
        in_specs=[pl.BlockSpec((bm, d), row_blk),
                  _layer_spec((d, 2 * width), layer, lambda i, j: (0, (j + MIX_FIRST_PAIR) % MIX_PAIRS)),
                  pl.BlockSpec((1, width), const2),
                  pl.BlockSpec((n_groups, SGU_CHUNK, SGU_CHUNK), const3),
                  pl.BlockSpec((n_groups, SGU_CHUNK, 1), const3),
                  pl.BlockSpec((CONV_TAPS, width), const2)],
        out_specs=[pl.BlockSpec((bm, 2 * width), row_blk),
                   pl.BlockSpec((bm, width), row_blk),
                   pl.BlockSpec((bm, width), row_blk),
                   pl.BlockSpec((bm, width), row_blk)],
        out_shape=[jax.ShapeDtypeStruct((m, 2 * width), BF16),
                   jax.ShapeDtypeStruct((m, width), BF16),
                   jax.ShapeDtypeStruct((m, width), BF16),
                   jax.ShapeDtypeStruct((m, width), BF16)],
        scratch_shapes=[pltpu.VMEM((bm, width), F32),
                        pltpu.VMEM((bm, width), BF16),
                        pltpu.VMEM((bm, width), F32),
                        pltpu.VMEM((bm, width), F32),
                        pltpu.VMEM((8, width), F32)],
        compiler_params=_params("arbitrary", "arbitrary"),
        name="mixer_in",
    )(n, w, sgu_norm.reshape(1, width), sgu_w_s, sgu_b.reshape(n_groups, SGU_CHUNK, 1), conv_w)


def _bf16_split3(x):
    p1 = x.astype(BF16).astype(F32)
    r1 = x - p1
    p2 = r1.astype(BF16).astype(F32)
    p3 = (r1 - p2).astype(BF16).astype(F32)
    return p1, p2, p3


BF16_SUBLANES = 16


def _diff_attn_kernel(scal_ref, q_ref, k_ref, v_ref, lam_ref, sub_ref, *rest, n_heads, tq, n_cast):
    cast_in, o_ref, cast_out = rest[:n_cast], rest[n_cast], rest[n_cast + 1:2 * n_cast + 1]
    kt_ref, vx_ref = rest[2 * n_cast + 1:]
    heads_here = kt_ref.shape[0]
    lam_init = scal_ref[n_heads]
    half = HEAD_DIM // 2
    seq = q_ref.shape[1]

    pos = lax.broadcasted_iota(jnp.int32, (HEAD_DIM, seq), 1)
    frow = lax.broadcasted_iota(jnp.int32, (HEAD_DIM, seq), 0)
    pos_hi = (pos - pos % HEAD_DIM).astype(F32)
    pos_lo = (pos % HEAD_DIM).astype(F32)
    pos_rows = jnp.where(frow < 3, pos_hi, jnp.where(frow < 6, pos_lo, 0.0)).astype(BF16)

    lp = lam_ref[...]
    lam = (jnp.exp(jnp.sum(lp[0:1] * lp[1:2], axis=-1, keepdims=True))
           - jnp.exp(jnp.sum(lp[2:3] * lp[3:4], axis=-1, keepdims=True)) + lam_init)
    sub = sub_ref[...]

    lane = lax.broadcasted_iota(jnp.int32, (tq, HEAD_DIM), 1)
    flane = lax.broadcasted_iota(jnp.int32, (2 * tq, HEAD_DIM), 1)
    r_idx = lax.broadcasted_iota(jnp.int32, (2 * tq, tq), 0)
    c_idx = lax.broadcasted_iota(jnp.int32, (2 * tq, tq), 1)
    causal = c_idx <= jnp.where(r_idx >= tq, r_idx - tq, r_idx)

    for hh in range(heads_here):
        cols = slice(hh * HEAD_DIM, (hh + 1) * HEAD_DIM)
        kt_ref[hh, :HEAD_DIM, :] = k_ref[0, :, cols].T
        kt_ref[hh, HEAD_DIM:, :] = pos_rows
        vx_ref[hh, :, :HEAD_DIM] = v_ref[0, :, cols]
        vx_ref[hh, :, HEAD_DIM:] = jnp.ones((seq, HEAD_DIM), BF16)
        slope = scal_ref[pl.program_id(1) * heads_here + hh]
        s1, s2, s3 = _bf16_split3(jnp.full((2 * tq, HEAD_DIM), slope, F32))
        piece = jnp.where(flane % 3 == 0, s1, jnp.where(flane % 3 == 1, s2, s3))
        q_feat = jnp.where(flane < 6, piece, 0.0).astype(BF16)

        for i in reversed(range(seq // tq)):
            start, stop = i * tq, (i + 1) * tq
            q = q_ref[0, start:stop, cols].astype(F32) * (1.0 / math.sqrt(half))
            qq = jnp.concatenate([jnp.where(lane < half, q, 0.0),
                                  jnp.where(lane >= half, q, 0.0)], axis=0).astype(BF16)
            t = _dot(jnp.concatenate([qq, q_feat], axis=1), kt_ref[hh, :, :stop])
            t_diag = jnp.where(causal, t[:, start:], NEG_BIG)
            m = jnp.max(t_diag, axis=-1, keepdims=True)
            if i:
                t_past = t[:, :start]
                m = jnp.maximum(m, jnp.max(t_past, axis=-1, keepdims=True))
                p = jnp.concatenate([jnp.exp(t_past - m), jnp.exp(t_diag - m)], axis=1)
            else:
                p = jnp.exp(t_diag - m)
            a = _dot(p.astype(BF16), vx_ref[hh, :stop, :])
            a = a[:, :HEAD_DIM] / a[:, HEAD_DIM:]
            o = a[:tq] - lam * a[tq:]
            o_ref[0, start:stop, cols] = (_rms_rows(o, sub) * (1.0 - lam_init)).astype(o_ref.dtype)

    for src, dst in zip(cast_in, cast_out):
        dst[...] = src[...].astype(dst.dtype)


def diff_attention(qk, v, lam_params, subln, lam_init, cast_stacks=(), cast_layer=0, *,
                   batch, seq, tq_target=256):
    width = v.shape[1]
    n_heads = width // HEAD_DIM
    tq = _pick(seq, tq_target)
    slopes = 2.0 ** (-8.0 * jnp.arange(1, n_heads + 1, dtype=F32) / n_heads)
    scal = jnp.concatenate([slopes, jnp.full((1,), lam_init, F32)])
    qk3 = qk.reshape(batch, seq, 2 * width)
    v3 = v.reshape(batch, seq, width)
    hps = _pick(n_heads, 2)
    groups = n_heads // hps
    head_blk = lambda off: pl.BlockSpec((1, seq, hps * HEAD_DIM), lambda b, h: (b, 0, off + h))

    steps = batch * groups
    cast_in_specs, cast_out_specs, cast_out_shapes = [], [], []
    for w in cast_stacks:
        _, rows, cols = w.shape
        n_blocks = steps
        while rows % n_blocks or (rows // n_blocks) % BF16_SUBLANES:
            n_blocks //= 2
        assert n_blocks >= 1 and steps % n_blocks == 0
        hold = steps // n_blocks
        blk = (None, rows // n_blocks, cols)
        cast_in_specs.append(pl.BlockSpec(blk, lambda b, h, hold=hold: (cast_layer, (b * groups + h) // hold, 0)))
        cast_out_specs.append(pl.BlockSpec(blk, lambda b, h, hold=hold: (0, (b * groups + h) // hold, 0)))
        cast_out_shapes.append(jax.ShapeDtypeStruct((1, rows, cols), BF16))

    out = pl.pallas_call(
        functools.partial(_diff_attn_kernel, n_heads=n_heads, tq=tq, n_cast=len(cast_stacks)),
        grid=(batch, groups),
        in_specs=[pl.BlockSpec(memory_space=pltpu.SMEM),
                  head_blk(0), head_blk(groups), head_blk(0),
                  pl.BlockSpec(lam_params.shape, lambda b, h: (0, 0)),
                  pl.BlockSpec((1, HEAD_DIM), lambda b, h: (0, 0))] + cast_in_specs,
        out_specs=[head_blk(0)] + cast_out_specs,
        out_shape=[jax.ShapeDtypeStruct((batch, seq, width), BF16)] + cast_out_shapes,
        scratch_shapes=[pltpu.VMEM((hps, 2 * HEAD_DIM, seq), BF16),
                        pltpu.VMEM((hps, seq, 2 * HEAD_DIM), BF16)],
        compiler_params=_params("arbitrary", "arbitrary"),
        name="diff_attention",
    )(scal, qk3, qk3, v3, lam_params, subln.reshape(1, HEAD_DIM), *cast_stacks)
    return out[0].reshape(batch * seq, width), list(out[1:])


def _merge_kernel(n_ref, wg0_ref, wg1_ref, wg2_ref, ya_ref, yb_ref, yc_ref,
                  wa_ref, wb_ref, wc_ref, o_ref):
    n = n_ref[...]

    def branch(wg_ref, y_ref, w_ref):
        return jax.nn.sigmoid(_dot(n, wg_ref[...])) * _dot(y_ref[...], w_ref[...])

    merged = branch(wg0_ref, ya_ref, wa_ref) + branch(wg1_ref, yb_ref, wb_ref)
    o_ref[...] = (merged + branch(wg2_ref, yc_ref, wc_ref)).astype(o_ref.dtype)


def gated_merge(n, w_mix_in, layer, y_a, y_b, y_c, w_a, w_b, w_c, *, bm_target=1024, bn_target=512):
    m, d = n.shape
    width = y_a.shape[1]
    bm, bn = _pick(m, bm_target), _pick(d, bn_target)
    nb = d // bn
    gate0 = (w_mix_in.shape[2] - N_BRANCHES * d) // bn
    row_blk = lambda i, j: (i, 0)
    col_blk = lambda i, j: (0, j)
    return pl.pallas_call(
        _merge_kernel,
        grid=(m // bm, nb),
        in_specs=[pl.BlockSpec((bm, d), row_blk),
                  _layer_spec((d, bn), layer, lambda i, j: (0, gate0 + j)),
                  _layer_spec((d, bn), layer, lambda i, j: (0, gate0 + nb + j)),
                  _layer_spec((d, bn), layer, lambda i, j: (0, gate0 + 2 * nb + j)),
                  pl.BlockSpec((bm, width), row_blk),
                  pl.BlockSpec((bm, width), row_blk),
                  pl.BlockSpec((bm, width), row_blk),
                  _layer_spec((width, bn), layer, col_blk),
                  _layer_spec((width, bn), layer, col_blk),
                  _layer_spec((width, bn), layer, col_blk)],
        out_specs=pl.BlockSpec((bm, bn), lambda i, j: (i, j)),
        out_shape=jax.ShapeDtypeStruct((m, d), BF16),
        compiler_params=_params("parallel", "arbitrary"),
        name="gated_merge",
    )(n, w_mix_in, w_mix_in, w_mix_in, y_a, y_b, y_c, w_a, w_b, w_c)


def _mem_attn_kernel(x_ref, merged_ref, wmix_ref, g_ref, wq_ref, mem_ref, gmem_ref, wkv_ref, wo_ref,
                     g_next_ref, o_ref, n_next_ref, kv_ref, *, per_seq):
    mem_width = wq_ref.shape[1]

    @pl.when(pl.program_id(0) % per_seq == 0)
    def _():
        mem_n = _rms_rows(mem_ref[0], gmem_ref[...]).astype(BF16)
        kv_ref[0] = _dot(mem_n, wkv_ref[...]).astype(BF16)

    x = x_ref[...] + _dot(merged_ref[...], wmix_ref[...])
    n = _rms_rows(x, g_ref[...]).astype(BF16)
    q = _dot(n, wq_ref[...]).astype(BF16)
    kv = kv_ref[0]
    scale = 1.0 / math.sqrt(HEAD_DIM)
    heads = []
    for hd in range(mem_width // HEAD_DIM):
        cols = slice(hd * HEAD_DIM, (hd + 1) * HEAD_DIM)
        k_h = kv[:, hd * HEAD_DIM:(hd + 1) * HEAD_DIM]
        v_h = kv[:, mem_width + hd * HEAD_DIM:mem_width + (hd + 1) * HEAD_DIM]
        s = _dot_t(q[:, cols], k_h) * scale
        p = jnp.exp(s - jnp.max(s, axis=-1, keepdims=True))
        denom = jnp.sum(p, axis=-1, keepdims=True)
        heads.append(_dot(p.astype(BF16), v_h) / denom)
    o = jnp.concatenate(heads, axis=1).astype(BF16)
    h = x + _dot(o, wo_ref[...])
    o_ref[...] = h
    n_next_ref[...] = _rms_rows(h, g_next_ref[...]).astype(n_next_ref.dtype)


def mixer_out_memory_attention(x, merged, w_mix_o, mix_layer, gain, w_q, mem, gain_mem, w_kv, w_o, layer,
                               gain_next, *, seq, bm_target=512):
    m, d = x.shape
    n_mem = mem.shape[1]
    mem_width = w_q.shape[2]
    bm = _pick(seq, bm_target)
    per_seq = seq // bm
    return pl.pallas_call(
        functools.partial(_mem_attn_kernel, per_seq=per_seq),
        grid=(m // bm,),
        in_specs=[pl.BlockSpec((bm, d), lambda i: (i, 0)),
                  pl.BlockSpec((bm, d), lambda i: (i, 0)),
                  _layer_spec((d, d), mix_layer, lambda i: (0, 0), single_buffer=True),
                  pl.BlockSpec((1, d), lambda i: (0, 0)),
                  _layer_spec((d, mem_width), layer, lambda i: (0, 0)),
                  pl.BlockSpec((1, n_mem, d), lambda i: (i // per_seq, 0, 0)),
                  pl.BlockSpec((1, d), lambda i: (0, 0)),
                  _layer_spec((d, 2 * mem_width), layer, lambda i: (0, 0), single_buffer=True),
                  _layer_spec((mem_width, d), layer, lambda i: (0, 0)),
                  pl.BlockSpec((1, d), lambda i: (0, 0))],
        out_specs=[pl.BlockSpec((bm, d), lambda i: (i, 0)),
                   pl.BlockSpec((bm, d), lambda i: (i, 0))],
        out_shape=[jax.ShapeDtypeStruct((m, d), F32),
                   jax.ShapeDtypeStruct((m, d), BF16)],
        scratch_shapes=[pltpu.VMEM((1, n_mem, 2 * mem_width), BF16)],
        compiler_params=_params("arbitrary"),
        name="memory_attention",
    )(x, merged, w_mix_o, gain.reshape(1, d), w_q, mem, gain_mem.reshape(1, d), w_kv, w_o,
      gain_next.reshape(1, d))


def kernel(x, mem, ffn1_norm, ffn1_w_in, ffn1_w_out, mix_norm, mix_w_in, diff_lambda, diff_subln, diff_w_out, sgu_norm, sgu_w_s, sgu_b, sgu_w_out, conv_w, conv_w_out, mix_w_o, xattn_norm, mem_norm, xattn_w_q, xattn_w_kv, xattn_w_o, ffn2_norm, ffn2_w_in, ffn2_w_out, final_norm):
    batch, seq, d = x.shape
    n_mem = mem.shape[1]
    depth = ffn1_norm.shape[0]
    h = x.reshape(batch * seq, d)
    mem2 = mem.reshape(batch * n_mem, d)

    big = dict(ffn1_w_in=ffn1_w_in, ffn1_w_out=ffn1_w_out, ffn2_w_in=ffn2_w_in, ffn2_w_out=ffn2_w_out,
               mix_w_in=mix_w_in, diff_w_out=diff_w_out, sgu_w_out=sgu_w_out, conv_w_out=conv_w_out,
               mix_w_o=mix_w_o)
    w = {name: stack[0:1].astype(BF16) for name, stack in big.items()}
    xattn_w_q, xattn_w_kv, xattn_w_o = (s.astype(BF16) for s in (xattn_w_q, xattn_w_kv, xattn_w_o))

    hidden = norm_swiglu_in(h, ffn1_norm[0], w["ffn1_w_in"], 0)
    for l in range(depth):
        lam_init = 0.8 - 0.6 * math.exp(-0.3 * l)
        last = l + 1 == depth

        h, n_mix = matmul_residual_norm(hidden, w["ffn1_w_out"], 0, h, 0.5, mix_norm[l])

        qk, v, y_b, y_c = mixer_in(n_mix, w["mix_w_in"], 0, sgu_norm[l], sgu_w_s[l], sgu_b[l], conv_w[l],
                                   seq=seq)
        y_a, casted = diff_attention(qk, v, diff_lambda[l], diff_subln[l], lam_init,
                                     () if last else tuple(big.values()), l + 1, batch=batch, seq=seq)
        merged = gated_merge(n_mix, w["mix_w_in"], 0, y_a, y_b, y_c,
                             w["diff_w_out"], w["sgu_w_out"], w["conv_w_out"])
        h, n_ffn2 = mixer_out_memory_attention(h, merged, w["mix_w_o"], 0, xattn_norm[l], xattn_w_q,
                                               mem, mem_norm[l], xattn_w_kv, xattn_w_o, l, ffn2_norm[l],
                                               seq=seq)

        hidden = swiglu_in(n_ffn2, w["ffn2_w_in"], 0)
        if last:
            out = matmul_residual_norm(hidden, w["ffn2_w_out"], 0, h, 0.5, final_norm, final=True)
        else:
            h, n_ffn1 = matmul_residual_norm(hidden, w["ffn2_w_out"], 0, h, 0.5, ffn1_norm[l + 1])
            w = dict(zip(big, casted))
            hidden = swiglu_in(n_ffn1, w["ffn1_w_in"], 0)

    return out.reshape(batch, seq, d)
```
